```python
import math
import jax, jax.numpy as jnp
from jax import lax
import numpy as np

D_MODEL = 1024
BATCH = 16
SEQ = 4096
DEPTH = 2
DEC_BATCH = 16
DEC_SEQ = 32
PAST_LEN = 1024

CHUNK = 64
N_MEM = 256
BRANCH = 2 * D_MODEL
MIX_W = 3 * D_MODEL // 2
MEM_W = D_MODEL // 2
MEM_HEADS = 4
MEM_HD = MEM_W // MEM_HEADS
RW_HD = 64
RW_HEADS = MIX_W // RW_HD
LORA = 64
RW_PROJ = 3 * MIX_W + 2 * LORA
RW_IN = RW_PROJ + MEM_W + BRANCH
DF_HD = 64
DF_HEADS = MIX_W // (2 * DF_HD)
DF_VD = 2 * DF_HD
DF_IN = 3 * MIX_W + MEM_W + BRANCH
Q_BLOCK = 128
N_RWKV = (DEPTH + 1) // 2
N_DIFF = DEPTH // 2
EPS = 1e-6
GN_EPS = 64e-5
NEG_INF = -1e30

kernel_name = 'hybrid_rwkv7_diffattn_stream_step'


def rmsnorm(x, w):
    xf = x.astype(jnp.float32)
    y = xf * lax.rsqrt(jnp.mean(xf * xf, axis=-1, keepdims=True) + EPS)
    return (y * w.astype(jnp.float32)).astype(x.dtype)


def mem_kv(mem, norm_w, wk, wv):
    B = mem.shape[0]
    h = rmsnorm(mem, norm_w)
    return ((h @ wk).reshape(B, N_MEM, MEM_HEADS, MEM_HD),
            (h @ wv).reshape(B, N_MEM, MEM_HEADS, MEM_HD))


def mem_attention(qm, mk, mv):
    B, T, _ = qm.shape
    q = qm.reshape(B, T, MEM_HEADS, MEM_HD)
    s = jnp.einsum('bthd,bmhd->bhtm', q, mk.astype(q.dtype)).astype(jnp.float32) * MEM_HD ** -0.5
    p = jax.nn.softmax(s, axis=-1).astype(mv.dtype)
    return jnp.einsum('bhtm,bmhd->bthd', p, mv).astype(qm.dtype).reshape(B, T, MEM_W)


def rwkv_mixer(p, prev_row, S0, mu, w0, w_up, a0, a_up, k_k, k_a, r_k, ln_w, ln_b):
    B, T, _ = p.shape
    f32 = jnp.float32
    p_prev = jnp.concatenate([prev_row[:, None, :].astype(p.dtype), p[:, :-1]], axis=1)
    ps = p + mu.astype(p.dtype) * (p_prev - p)
    r, k, v, pw, pa = jnp.split(ps, [MIX_W, 2 * MIX_W, 3 * MIX_W, 3 * MIX_W + LORA], axis=-1)
    log_w = -jax.nn.softplus(-(w0 + jnp.tanh(pw) @ w_up).astype(f32)) - 0.5
    a = jax.nn.sigmoid((a0 + pa @ a_up).astype(f32))
    heads = lambda t: t.astype(f32).reshape(B, T, RW_HEADS, RW_HD)
    r, k, v, decay, a = heads(r), heads(k), heads(v), heads(jnp.exp(-jnp.exp(log_w))), heads(a)
    kk = k * k_k.astype(f32).reshape(RW_HEADS, RW_HD)
    kk = kk / jnp.maximum(jnp.linalg.norm(kk, axis=-1, keepdims=True), 1e-12)
    k = k * (1.0 + (a - 1.0) * k_a.astype(f32).reshape(RW_HEADS, RW_HD))

    def step(S, inp):
        r_t, k_t, v_t, w_t, kk_t, a_t = inp
        sa = jnp.einsum('bhvk,bhk->bhv', S, -kk_t)
        S = (S * w_t[:, :, None, :] + sa[..., None] * (kk_t * a_t)[:, :, None, :]
             + v_t[..., None] * k_t[:, :, None, :])
        return S, jnp.einsum('bhvk,bhk->bhv', S, r_t)

    xs = tuple(jnp.swapaxes(t, 0, 1) for t in (r, k, v, decay, kk, a))
    S_T, y = lax.scan(step, S0.astype(f32), xs)
    y = jnp.swapaxes(y, 0, 1)
    mean = jnp.mean(y, axis=-1, keepdims=True)
    var = jnp.mean(jnp.square(y - mean), axis=-1, keepdims=True)
    y = ((y - mean) * lax.rsqrt(var + GN_EPS) * ln_w.astype(f32).reshape(RW_HEADS, RW_HD)
         + ln_b.astype(f32).reshape(RW_HEADS, RW_HD))
    y = y + jnp.sum(r * k * r_k.astype(f32), axis=-1, keepdims=True) * v
    return y.reshape(B, T, MIX_W), S_T, p[:, -1]


def rwkv_layer(x, mk, mv, prev_row, S0, norm_w, w_out, w_in, mu, w0, w_up, a0, a_up,
               k_k, k_a, r_k, ln_w, ln_b):
    h = rmsnorm(x, norm_w)
    p, qm, gate = jnp.split(h @ w_in, [RW_PROJ, RW_PROJ + MEM_W], axis=-1)
    y_mix, S_T, last_row = rwkv_mixer(p, prev_row, S0, mu, w0, w_up, a0, a_up, k_k, k_a, r_k, ln_w, ln_b)
    y_mem = mem_attention(qm, mk, mv)
    out = jnp.concatenate([y_mix.astype(x.dtype), y_mem], axis=-1) * jax.nn.silu(gate)
    return x + out @ w_out, S_T, last_row


def diff_attention(q, k_all, v_all, q_pos, k_pos, lam):
    q1, q2 = jnp.split(q, 2, axis=-1)
    k1, k2 = jnp.split(k_all, 2, axis=-1)
    mask = k_pos[None, :] < (q_pos[:, None] // CHUNK + 1) * CHUNK

    def probs(qa, ka):
        s = jnp.einsum('bqhd,bkhd->bhqk', qa, ka).astype(jnp.float32) * DF_HD ** -0.5
        return jax.nn.softmax(jnp.where(mask, s, NEG_INF), axis=-1)

    attn = probs(q1, k1) - lam * probs(q2, k2)
    return jnp.einsum('bhqk,bkhd->bqhd', attn.astype(v_all.dtype), v_all)


def diff_layer(x, mk, mv, k_past, v_past, layer_idx, norm_w, w_out, w_in, lq1, lk1, lq2, lk2, subln):
    B, T, _ = x.shape
    h = rmsnorm(x, norm_w)
    q, k, v, qm, gate = jnp.split(h @ w_in, [MIX_W, 2 * MIX_W, 3 * MIX_W, 3 * MIX_W + MEM_W], axis=-1)
    q = q.reshape(B, T, DF_HEADS, 2 * DF_HD)
    k = k.reshape(B, T, DF_HEADS, 2 * DF_HD)
    v = v.reshape(B, T, DF_HEADS, DF_VD)
    if k_past is None:
        past, k_all, v_all = 0, k, v
    else:
        past = k_past.shape[1]
        k_all = jnp.concatenate([k_past.astype(k.dtype), k], axis=1)
        v_all = jnp.concatenate([v_past.astype(v.dtype), v], axis=1)
    lam_init = 0.8 - 0.6 * math.exp(-0.3 * layer_idx)
    f32 = jnp.float32
    lam = (jnp.exp(jnp.sum(lq1.astype(f32) * lk1.astype(f32)))
           - jnp.exp(jnp.sum(lq2.astype(f32) * lk2.astype(f32))) + lam_init)
    blocks = []
    for s in range(0, T, Q_BLOCK):
        e = min(s + Q_BLOCK, T)
        q_pos = past + jnp.arange(s, e)
        k_pos = jnp.arange(past + e)
        blocks.append(diff_attention(q[:, s:e], k_all[:, :past + e], v_all[:, :past + e], q_pos, k_pos, lam))
    o = jnp.concatenate(blocks, axis=1)
    o = rmsnorm(o, subln) * (1.0 - lam_init)
    y_mem = mem_attention(qm, mk, mv)
    out = jnp.concatenate([o.reshape(B, T, MIX_W).astype(x.dtype), y_mem], axis=-1) * jax.nn.silu(gate)
    return x + out @ w_out, k, v


def setup_inputs(seed: int = 0) -> dict:
    key = jax.random.key(seed)
    ks = iter(jax.random.split(key, 40))
    f32 = jnp.float32
    nrm = lambda shape, scale: jax.random.normal(next(ks), shape, f32) * scale
    uni = lambda shape, lo, hi: jax.random.uniform(next(ks), shape, f32, lo, hi)
    return {
        'x_prompt': nrm((BATCH, SEQ, D_MODEL), 1.0),
        'mem_prompt': nrm((BATCH, N_MEM, D_MODEL), 1.0),
        'x_sample': nrm((DEC_BATCH, DEC_SEQ, D_MODEL), 1.0),
        'state_rwkv': nrm((N_RWKV, DEC_BATCH, RW_HEADS, RW_HD, RW_HD), 0.1),
        'state_shift': nrm((N_RWKV, DEC_BATCH, RW_PROJ), 1.0),
        'cache_k': nrm((N_DIFF, DEC_BATCH, PAST_LEN, DF_HEADS, 2 * DF_HD), 1.0),
        'cache_v': nrm((N_DIFF, DEC_BATCH, PAST_LEN, DF_HEADS, DF_VD), 1.0),
        'cache_mem_k': nrm((DEPTH, DEC_BATCH, N_MEM, MEM_HEADS, MEM_HD), 1.0),
        'cache_mem_v': nrm((DEPTH, DEC_BATCH, N_MEM, MEM_HEADS, MEM_HD), 1.0),
        'norm_w': 1.0 + nrm((DEPTH, D_MODEL), 0.02),
        'mem_norm_w': 1.0 + nrm((DEPTH, D_MODEL), 0.02),
        'w_mem_k': nrm((DEPTH, D_MODEL, MEM_W), D_MODEL ** -0.5),
        'w_mem_v': nrm((DEPTH, D_MODEL, MEM_W), D_MODEL ** -0.5),
        'w_out': nrm((DEPTH, BRANCH, D_MODEL), 0.5 * BRANCH ** -0.5),
        'final_norm_w': 1.0 + nrm((D_MODEL,), 0.02),
        'rw_in': nrm((N_RWKV, D_MODEL, RW_IN), D_MODEL ** -0.5),
        'rw_mu': uni((N_RWKV, RW_PROJ), 0.0, 1.0),
        'rw_w0': uni((N_RWKV, MIX_W), -6.0, 0.0),
        'rw_w_up': nrm((N_RWKV, LORA, MIX_W), 0.1),
        'rw_a0': nrm((N_RWKV, MIX_W), 0.5),
        'rw_a_up': nrm((N_RWKV, LORA, MIX_W), 0.1),
        'rw_k_k': 0.85 + nrm((N_RWKV, MIX_W), 0.02),
        'rw_k_a': 1.0 + nrm((N_RWKV, MIX_W), 0.02),
        'rw_r_k': nrm((N_RWKV, RW_HEADS, RW_HD), 0.1),
        'rw_ln_w': 1.0 + nrm((N_RWKV, MIX_W), 0.02),
        'rw_ln_b': nrm((N_RWKV, MIX_W), 0.02),
        'df_in': nrm((N_DIFF, D_MODEL, DF_IN), D_MODEL ** -0.5),
        'df_lq1': nrm((N_DIFF, DF_HD), 0.1),
        'df_lk1': nrm((N_DIFF, DF_HD), 0.1),
        'df_lq2': nrm((N_DIFF, DF_HD), 0.1),
        'df_lk2': nrm((N_DIFF, DF_HD), 0.1),
        'df_subln': 1.0 + nrm((N_DIFF, DF_VD), 0.02),
    }


def reference(x_prompt, mem_prompt, x_sample, state_rwkv, state_shift, cache_k, cache_v,
              cache_mem_k, cache_mem_v, norm_w, mem_norm_w, w_mem_k, w_mem_v, w_out, final_norm_w,
              rw_in, rw_mu, rw_w0, rw_w_up, rw_a0, rw_a_up, rw_k_k, rw_k_a, rw_r_k, rw_ln_w, rw_ln_b,
              df_in, df_lq1, df_lk1, df_lq2, df_lk2, df_subln):
    xp, xs = x_prompt, x_sample
    Bp = xp.shape[0]
    p_S, p_shift, p_k, p_v, p_mk, p_mv = [], [], [], [], [], []
    s_S, s_shift, s_k, s_v = [], [], [], []
    for i in range(DEPTH):
        j = i // 2
        mkp, mvp = mem_kv(mem_prompt, mem_norm_w[i], w_mem_k[i], w_mem_v[i])
        p_mk.append(mkp)
        p_mv.append(mvp)
        if i % 2 == 0:
            rw = (rw_in[j], rw_mu[j], rw_w0[j], rw_w_up[j], rw_a0[j], rw_a_up[j],
                  rw_k_k[j], rw_k_a[j], rw_r_k[j], rw_ln_w[j], rw_ln_b[j])
            zero_row = jnp.zeros((Bp, RW_PROJ), xp.dtype)
            zero_S = jnp.zeros((Bp, RW_HEADS, RW_HD, RW_HD), jnp.float32)
            xp, Sp, rp = rwkv_layer(xp, mkp, mvp, zero_row, zero_S, norm_w[i], w_out[i], *rw)
            xs, Ss, rs = rwkv_layer(xs, cache_mem_k[i], cache_mem_v[i], state_shift[j], state_rwkv[j],
                                    norm_w[i], w_out[i], *rw)
            p_S.append(Sp)
            p_shift.append(rp)
            s_S.append(Ss)
            s_shift.append(rs)
        else:
            df = (df_in[j], df_lq1[j], df_lk1[j], df_lq2[j], df_lk2[j], df_subln[j])
            xp, kp, vp = diff_layer(xp, mkp, mvp, None, None, i, norm_w[i], w_out[i], *df)
            xs, ks_, vs_ = diff_layer(xs, cache_mem_k[i], cache_mem_v[i], cache_k[j], cache_v[j], i,
                                      norm_w[i], w_out[i], *df)
            p_k.append(kp)
            p_v.append(vp)
            s_k.append(ks_)
            s_v.append(vs_)
    y_prompt = rmsnorm(xp, final_norm_w)
    y_sample = rmsnorm(xs, final_norm_w)
    return (y_prompt, y_sample, jnp.stack(p_S), jnp.stack(p_shift), jnp.stack(p_k), jnp.stack(p_v),
            jnp.stack(p_mk), jnp.stack(p_mv), jnp.stack(s_S), jnp.stack(s_shift), jnp.stack(s_k), jnp.stack(s_v))
```

```python
import functools
import math

import jax
import jax.numpy as jnp
from jax import lax
from jax.experimental import pallas as pl
from jax.experimental.pallas import tpu as pltpu

D_MODEL = 1024
CHUNK = 64
CHUNK_SHIFT = 6
N_MEM = 256
BRANCH = 2 * D_MODEL
MIX_W = 3 * D_MODEL // 2
MEM_W = D_MODEL // 2
MEM_HEADS = 4
MEM_HD = MEM_W // MEM_HEADS
RW_HD = 64
RW_HEADS = MIX_W // RW_HD
LORA = 64
RW_PROJ = 3 * MIX_W + 2 * LORA
DF_HD = 64
DF_HEADS = MIX_W // (2 * DF_HD)
DF_VD = 2 * DF_HD
EPS = 1e-6
GN_EPS = 64e-5
NEG_INF = -1e30

LANES = 128
RW_PAIRS = RW_HEADS // 2
RW_PROJ_PAD = 4864
V7X_VMEM_LIMIT = 56 * 1024 * 1024

BF16 = jnp.bfloat16
F32 = jnp.float32


def _mm(a, b):
    return jnp.dot(a.astype(BF16), b.astype(BF16), preferred_element_type=F32)


def _mm_nt(a, b):
    return lax.dot_general(a.astype(BF16), b.astype(BF16), (((1,), (1,)), ((), ())),
                           preferred_element_type=F32)


def _mm_tn(a, b):
    return lax.dot_general(a.astype(BF16), b.astype(BF16), (((0,), (0,)), ((), ())),
                           preferred_element_type=F32)


def _sigmoid(x):
    return 1.0 / (1.0 + jnp.exp(-x))


def _softplus(x):
    return jnp.maximum(x, 0.0) + jnp.log(1.0 + jnp.exp(-jnp.abs(x)))


def _params(sem):
    return pltpu.CompilerParams(dimension_semantics=sem, vmem_limit_bytes=V7X_VMEM_LIMIT)


def _norm_proj_kernel(x_ref, nw_ref, w_ref, o_ref, h_ref):
    @pl.when(pl.program_id(1) == 0)
    def _():
        x = x_ref[...]
        ms = jnp.mean(x * x, axis=-1, keepdims=True)
        h_ref[...] = (x * lax.rsqrt(ms + EPS) * nw_ref[...]).astype(BF16)

    o_ref[...] = jnp.dot(h_ref[...], w_ref[...], preferred_element_type=F32)


def norm_proj(x, norm_w, w, bn):
    n, d = x.shape
    nout = w.shape[1]
    bm = min(512, n)
    assert n % bm == 0 and nout % bn == 0
    return pl.pallas_call(
        _norm_proj_kernel,
        grid=(n // bm, nout // bn),
        in_specs=[pl.BlockSpec((bm, d), lambda i, j: (i, 0)),
                  pl.BlockSpec((1, d), lambda i, j: (0, 0)),
                  pl.BlockSpec((d, bn), lambda i, j: (0, j))],
        out_specs=pl.BlockSpec((bm, bn), lambda i, j: (i, j)),
        out_shape=jax.ShapeDtypeStruct((n, nout), F32),
        scratch_shapes=[pltpu.VMEM((bm, d), BF16)],
        compiler_params=_params(("parallel", "arbitrary")),
        name="norm_proj",
    )(x, norm_w.reshape(1, d), w)


def _split3(x):
    hi = x.astype(BF16)
    r1 = x - hi.astype(F32)
    mid = r1.astype(BF16)
    lo = (r1 - mid.astype(F32)).astype(BF16)
    return hi, mid, lo


def _rwkv_kernel(r_ref, k_ref, v_ref, pp_ref, pr_r, pr_k, pr_v, pr_pp,
                 mu_r, mu_k, mu_v, mu_pp, w0_ref, a0_ref, kk_ref, ka_ref, rk_ref,
                 lnw_ref, lnb_ref, wup_ref, aup_ref, z0_ref,
                 y_ref, zout_ref,
                 z_scr, c_r, c_k, c_v, c_pp, *, ts, cl):
    nc = ts // cl
    rb = 2 * cl
    c_idx = pl.program_id(2)

    @pl.when(c_idx == 0)
    def _init():
        z_scr[...] = z0_ref[0, 0]
        c_r[...] = pr_r[0]
        c_k[...] = pr_k[0]
        c_v[...] = pr_v[0]
        c_pp[...] = pr_pp[0]

    row_ts = lax.broadcasted_iota(jnp.int32, (ts, LANES), 0)

    def shifted(x, carry_ref, mu):
        prev = jnp.where(row_ts == 0, carry_ref[...], pltpu.roll(x, 1, 0))
        carry_ref[...] = x[ts - 1:ts, :]
        return x + mu * (prev - x)

    r = shifted(r_ref[0], c_r, mu_r[...])
    k = shifted(k_ref[0], c_k, mu_k[...])
    v = shifted(v_ref[0], c_v, mu_v[...])
    pp = shifted(pp_ref[0], c_pp, mu_pp[...])

    lw = w0_ref[...] + _mm(jnp.tanh(pp), wup_ref[...])
    log_w = -_softplus(-lw) - 0.5
    d = -jnp.exp(log_w)
    a = _sigmoid(a0_ref[...] + _mm(pp, aup_ref[...]))

    li = lax.broadcasted_iota(jnp.int32, (LANES, LANES), 0)
    lj = lax.broadcasted_iota(jnp.int32, (LANES, LANES), 1)
    seg = ((li >> 6) == (lj >> 6)).astype(BF16)

    kk = k * kk_ref[...]
    kk = kk / jnp.maximum(jnp.sqrt(_mm(kk * kk, seg)), 1e-12)
    k2 = k * (1.0 + (a - 1.0) * ka_ref[...])
    alpha = -kk
    b = kk * a
    bonus = _mm(r * k2 * rk_ref[...], seg) * v

    ci = lax.broadcasted_iota(jnp.int32, (cl, cl), 0)
    cj = lax.broadcasted_iota(jnp.int32, (cl, cl), 1)
    tril = (ci >= cj).astype(BF16)
    lane_c = lax.broadcasted_iota(jnp.int32, (cl, LANES), 1)
    m1 = (lane_c < RW_HD).astype(F32)
    m2 = 1.0 - m1
    si = lax.broadcasted_iota(jnp.int32, (rb, rb), 0)
    sj = lax.broadcasted_iota(jnp.int32, (rb, rb), 1)
    strict = si > sj
    incl = si >= sj
    eye = si == sj

    def level_mask(sh):
        return (((si >> (sh + 1)) == (sj >> (sh + 1)))
                & (((si >> sh) & 1) == 1) & (((sj >> sh) & 1) == 0))

    n_levels = int(math.log2(cl))

    def stack(x):
        return jnp.concatenate([x * m1, x * m2], axis=0)

    def prepass(c0):
        sl = slice(c0 * cl, (c0 + 1) * cl)
        dd = d[sl]
        hi, mid, lo = _split3(dd)
        cs = (jnp.dot(tril, hi, preferred_element_type=F32)
              + jnp.dot(tril, mid, preferred_element_type=F32)
              + jnp.dot(tril, lo, preferred_element_type=F32))
        e_c = jnp.exp(cs)
        e_nc = jnp.exp(-cs)
        p_c = jnp.exp(cs[cl - 1:cl, :])
        at = alpha[sl] * jnp.exp(cs - dd)
        rt = r[sl] * e_c
        bt = b[sl] * e_nc
        kt = k2[sl] * e_nc
        ats, rts, bts, kts = stack(at), stack(rt), stack(bt), stack(kt)
        bhs, khs, vs = stack(bt * p_c), stack(kt * p_c), stack(v[sl])
        if rb == LANES:
            g = _mm_nt(jnp.concatenate([ats, rts], axis=0), jnp.concatenate([bts, kts], axis=0))
            g_ab, g_ak, g_rb, g_rk = g[:rb, :rb], g[:rb, rb:], g[rb:, :rb], g[rb:, rb:]
        else:
            g_ab, g_ak = _mm_nt(ats, bts), _mm_nt(ats, kts)
            g_rb, g_rk = _mm_nt(rts, bts), _mm_nt(rts, kts)
        a_ab = jnp.where(strict, g_ab, 0.0)
        a_ak = jnp.where(strict, g_ak, 0.0)
        a_rb = jnp.where(incl, g_rb, 0.0)
        a_rk = jnp.where(incl, g_rk, 0.0)
        t = jnp.where(eye, 1.0, 0.0) + jnp.where(level_mask(0), a_ab, 0.0)
        for sh in range(1, n_levels):
            t = t + _mm(_mm(t, jnp.where(level_mask(sh), a_ab, 0.0)), t)
        ap = _mm(t, ats)
        vp = _mm(t, _mm(a_ak, vs))
        rp = rts + _mm(a_rb, ap)
        if rb == LANES:
            vv = jnp.concatenate([vp, vs], axis=0)
            yp = _mm(jnp.concatenate([a_rb, a_rk], axis=1), vv)
            nn = _mm_tn(jnp.concatenate([bhs, khs], axis=0), vv)
        else:
            yp = _mm(a_rb, vp) + _mm(a_rk, vs)
            nn = _mm_tn(bhs, vp) + _mm_tn(khs, vs)
        mm = jnp.where(li == lj, jnp.broadcast_to(p_c, (LANES, LANES)), 0.0) + _mm_tn(bhs, ap)
        return rp, yp, mm, nn

    pre = [prepass(c0) for c0 in range(nc)]

    z = z_scr[...]
    ys = []
    for rp, yp, mm, nn in pre:
        y_st = _mm(rp, z) + yp
        ys.append(y_st[:cl] + y_st[cl:])
        z = _mm(mm, z) + nn
    z_scr[...] = z
    y = ys[0] if nc == 1 else jnp.concatenate(ys, axis=0)

    inv_hd = 1.0 / RW_HD
    mean = _mm(y, seg) * inv_hd
    yc = y - mean
    var = _mm(yc * yc, seg) * inv_hd
    y_ref[0] = yc * lax.rsqrt(var + GN_EPS) * lnw_ref[...] + lnb_ref[...] + bonus

    @pl.when(c_idx == pl.num_programs(2) - 1)
    def _fin():
        zout_ref[0, 0] = z


def rwkv_core(p, prev_row, z0, mu, w0, a0, k_k, k_a, r_k, ln_w, ln_b, wup, aup, ts, cl):
    bsz, t, _ = p.shape
    assert t % ts == 0 and ts % cl == 0
    nb = MIX_W // LANES
    tok = lambda off: pl.BlockSpec((1, ts, LANES), lambda b, j, c, off=off: (b, c, off + j))
    prev = lambda off: pl.BlockSpec((1, 1, LANES), lambda b, j, c, off=off: (b, 0, off + j))
    mus = lambda off: pl.BlockSpec((1, LANES), lambda b, j, c, off=off: (0, off + j))
    fixed = lambda off: (lambda b, j, c: (b, c, off))
    chan = pl.BlockSpec((1, LANES), lambda b, j, c: (0, j))
    lora = pl.BlockSpec((LANES, LANES), lambda b, j, c: (0, j))
    state = pl.BlockSpec((1, 1, LANES, LANES), lambda b, j, c: (b, j, 0, 0))
    in_specs = [tok(0), tok(nb), tok(2 * nb),
                pl.BlockSpec((1, ts, LANES), lambda b, j, c: (b, c, 3 * nb)),
                prev(0), prev(nb), prev(2 * nb),
                pl.BlockSpec((1, 1, LANES), lambda b, j, c: (b, 0, 3 * nb)),
                mus(0), mus(nb), mus(2 * nb),
                pl.BlockSpec((1, LANES), lambda b, j, c: (0, 3 * nb)),
                chan, chan, chan, chan, chan, chan, chan, lora, lora, state]
    del fixed
    row = lambda a: a.reshape(1, -1)
    return pl.pallas_call(
        functools.partial(_rwkv_kernel, ts=ts, cl=cl),
        grid=(bsz, RW_PAIRS, t // ts),
        in_specs=in_specs,
        out_specs=[pl.BlockSpec((1, ts, LANES), lambda b, j, c: (b, c, j)), state],
        out_shape=[jax.ShapeDtypeStruct((bsz, t, MIX_W), F32),
                   jax.ShapeDtypeStruct((bsz, RW_PAIRS, LANES, LANES), F32)],
        scratch_shapes=[pltpu.VMEM((LANES, LANES), F32)] + [pltpu.VMEM((1, LANES), F32)] * 4,
        compiler_params=_params(("parallel", "parallel", "arbitrary")),
        name="rwkv_core",
    )(p, p, p, p, prev_row, prev_row, prev_row, prev_row, row(mu), row(mu), row(mu), row(mu),
      row(w0), row(a0), row(k_k), row(k_a), row(r_k), row(ln_w), row(ln_b), wup, aup, z0)


def _pair_state_in(s):
    bsz = s.shape[0]
    st = jnp.swapaxes(s.astype(F32), -1, -2).reshape(bsz, RW_PAIRS, 2, RW_HD, RW_HD)
    z = st[:, :, :, :, None, :] * jnp.eye(2, dtype=F32)[None, None, :, None, :, None]
    return z.reshape(bsz, RW_PAIRS, LANES, LANES)


def _pair_state_out(z):
    bsz = z.shape[0]
    z6 = z.reshape(bsz, RW_PAIRS, 2, RW_HD, 2, RW_HD)
    st = jnp.stack([z6[:, :, 0, :, 0, :], z6[:, :, 1, :, 1, :]], axis=2)
    return jnp.swapaxes(st.reshape(bsz, RW_HEADS, RW_HD, RW_HD), -1, -2)


def _diff_attn_kernel(q_ref, k_ref, v_ref, lq1, lk1, lq2, lk2, sub_ref, o_ref, kb, vb,
                      *, tq, tk, t_kv, past, lam_init):
    qi = pl.program_id(2)

    @pl.when(qi == 0)
    def _cast():
        kb[...] = k_ref[0].astype(BF16)
        vb[...] = v_ref[0].astype(BF16)

    lam = (jnp.exp(jnp.sum(lq1[...] * lk1[...], axis=-1, keepdims=True))
           - jnp.exp(jnp.sum(lq2[...] * lk2[...], axis=-1, keepdims=True)) + lam_init)

    lane = lax.broadcasted_iota(jnp.int32, (tq, LANES), 1)
    q = q_ref[0] * (DF_HD ** -0.5)
    q1 = jnp.where(lane < DF_HD, q, 0.0).astype(BF16)
    q2 = jnp.where(lane >= DF_HD, q, 0.0).astype(BF16)

    q0 = past + qi * tq
    lim_first = (q0 // CHUNK + 1) * CHUNK
    lim_last = ((q0 + tq - 1) // CHUNK + 1) * CHUNK
    n_kv = t_kv // tk
    n_full = jnp.minimum(lim_first // tk, n_kv)
    n_tot = jnp.minimum((lim_last + tk - 1) // tk, n_kv)

    row_pos = q0 + lax.broadcasted_iota(jnp.int32, (tq, tk), 0)
    row_lim = ((row_pos >> CHUNK_SHIFT) + 1) << CHUNK_SHIFT
    col = lax.broadcasted_iota(jnp.int32, (tq, tk), 1)

    def step(j, carry, masked):
        start = pl.multiple_of(j * tk, tk)
        kc = kb[pl.ds(start, tk), :]
        vc = vb[pl.ds(start, tk), :]
        new = []
        for qh, (m, l, acc) in zip((q1, q2), carry):
            s = lax.dot_general(qh, kc, (((1,), (1,)), ((), ())), preferred_element_type=F32)
            if masked:
                s = jnp.where(col + start < row_lim, s, NEG_INF)
            m_new = jnp.maximum(m, jnp.max(s, axis=-1, keepdims=True))
            corr = jnp.exp(m - m_new)
            p = jnp.exp(s - m_new)
            l = l * corr + jnp.sum(p, axis=-1, keepdims=True)
            acc = acc * corr + jnp.dot(p.astype(BF16), vc, preferred_element_type=F32)
            new.append((m_new, l, acc))
        return tuple(new)

    init1 = (jnp.full((tq, 1), NEG_INF, F32), jnp.zeros((tq, 1), F32), jnp.zeros((tq, LANES), F32))
    carry = (init1, init1)
    carry = lax.fori_loop(0, n_full, functools.partial(step, masked=False), carry)
    carry = lax.fori_loop(n_full, n_tot, functools.partial(step, masked=True), carry)
    (_, l1, acc1), (_, l2, acc2) = carry
    o = acc1 / l1 - lam * (acc2 / l2)
    ms = jnp.mean(o * o, axis=-1, keepdims=True)
    o_ref[0] = o * lax.rsqrt(ms + EPS) * sub_ref[...] * (1.0 - lam_init)


def diff_attn(q, k_all, v_all, lq1, lk1, lq2, lk2, subln, lam_init, tq, tk):
    bsz, t_q, _ = q.shape
    t_kv = k_all.shape[1]
    assert t_q % tq == 0 and t_kv % tk == 0
    kern = functools.partial(_diff_attn_kernel, tq=tq, tk=tk, t_kv=t_kv, past=t_kv - t_q, lam_init=lam_init)
    vec = pl.BlockSpec((1, DF_HD), lambda b, h, i: (0, 0))
    kv = pl.BlockSpec((1, t_kv, LANES), lambda b, h, i: (b, 0, h))
    row = lambda a: a.reshape(1, -1)
    return pl.pallas_call(
        kern,
        grid=(bsz, DF_HEADS, t_q // tq),
        in_specs=[pl.BlockSpec((1, tq, LANES), lambda b, h, i: (b, i, h)), kv, kv,
                  vec, vec, vec, vec, pl.BlockSpec((1, DF_VD), lambda b, h, i: (0, 0))],
        out_specs=pl.BlockSpec((1, tq, LANES), lambda b, h, i: (b, i, h)),
        out_shape=jax.ShapeDtypeStruct((bsz, t_q, MIX_W), F32),
        scratch_shapes=[pltpu.VMEM((t_kv, LANES), BF16), pltpu.VMEM((t_kv, LANES), BF16)],
        compiler_params=_params(("parallel", "parallel", "arbitrary")),
        name="diff_attn",
    )(q, k_all, v_all, row(lq1), row(lk1), row(lq2), row(lk2), row(subln))


def _gate_out_kernel(x_ref, ymix_ref, gq_ref, mk_ref, mv_ref, wout_ref, fw_ref, o_ref, *, final):
    gq = gq_ref[0]
    gate = gq[:, :BRANCH]
    heads = []
    for h in range(MEM_HEADS):
        sl = slice(h * MEM_HD, (h + 1) * MEM_HD)
        qh = gq[:, BRANCH + h * MEM_HD:BRANCH + (h + 1) * MEM_HD]
        s = _mm_nt(qh, mk_ref[0][:, sl]) * (MEM_HD ** -0.5)
        e = jnp.exp(s - jnp.max(s, axis=-1, keepdims=True))
        prob = e / jnp.sum(e, axis=-1, keepdims=True)
        heads.append(_mm(prob, mv_ref[0][:, sl]))
    branch = jnp.concatenate([ymix_ref[0]] + heads, axis=-1)
    act = (branch * (gate * _sigmoid(gate))).astype(BF16)
    xn = x_ref[0] + jnp.dot(act, wout_ref[...], preferred_element_type=F32)
    if final:
        ms = jnp.mean(xn * xn, axis=-1, keepdims=True)
        xn = xn * lax.rsqrt(ms + EPS) * fw_ref[...]
    o_ref[0] = xn


def gate_out(x, ymix, gq, mk, mv, w_out, final_w, final):
    bsz, t, d = x.shape
    bm = min(512, t)
    assert t % bm == 0
    tokb = lambda w: pl.BlockSpec((1, bm, w), lambda b, i: (b, i, 0))
    mem = pl.BlockSpec((1, N_MEM, MEM_W), lambda b, i: (b, 0, 0))
    return pl.pallas_call(
        functools.partial(_gate_out_kernel, final=final),
        grid=(bsz, t // bm),
        in_specs=[tokb(d), tokb(MIX_W), tokb(BRANCH + MEM_W), mem, mem,
                  pl.BlockSpec((BRANCH, d), lambda b, i: (0, 0)),
                  pl.BlockSpec((1, d), lambda b, i: (0, 0))],
        out_specs=tokb(d),
        out_shape=jax.ShapeDtypeStruct((bsz, t, d), F32),
        compiler_params=_params(("parallel", "arbitrary")),
        name="gate_out",
    )(x, ymix, gq, mk, mv, w_out, final_w.reshape(1, d))


def _mem_kv(mem, norm_w, wk, wv):
    bsz = mem.shape[0]
    flat = mem.reshape(bsz * N_MEM, D_MODEL)
    mk = norm_proj(flat, norm_w, wk.astype(BF16), MEM_W)
    mv = norm_proj(flat, norm_w, wv.astype(BF16), MEM_W)
    return mk.reshape(bsz, N_MEM, MEM_W), mv.reshape(bsz, N_MEM, MEM_W)


def _rwkv_layer(x, mk, mv, prev_row, s0, norm_w, w_out, final_w, w_in, mu, w0, w_up, a0, a_up,
                k_k, k_a, r_k, ln_w, ln_b, ts, cl):
    bsz, t, d = x.shape
    flat = x.reshape(bsz * t, d)
    w_p = jnp.pad(w_in[:, :RW_PROJ], ((0, 0), (0, RW_PROJ_PAD - RW_PROJ))).astype(BF16)
    w_gq = jnp.concatenate([w_in[:, RW_PROJ + MEM_W:], w_in[:, RW_PROJ:RW_PROJ + MEM_W]], axis=1).astype(BF16)
    p = norm_proj(flat, norm_w, w_p, RW_PROJ_PAD // 2).reshape(bsz, t, RW_PROJ_PAD)
    gq = norm_proj(flat, norm_w, w_gq, (BRANCH + MEM_W) // 2).reshape(bsz, t, BRANCH + MEM_W)
    zero = jnp.zeros((LORA, MIX_W), F32)
    wup = jnp.concatenate([w_up, zero], axis=0).astype(BF16)
    aup = jnp.concatenate([zero, a_up], axis=0).astype(BF16)
    y_mix, z_t = rwkv_core(p, prev_row.reshape(bsz, 1, RW_PROJ), _pair_state_in(s0), mu, w0, a0, k_k, k_a,
                           r_k.reshape(-1), ln_w, ln_b, wup, aup, ts, cl)
    x_new = gate_out(x, y_mix, gq, mk, mv, w_out.astype(BF16), final_w, final=False)
    return x_new, _pair_state_out(z_t), p[:, t - 1, :RW_PROJ]


def _diff_layer(x, mk, mv, k_past, v_past, layer_idx, norm_w, w_out, final_w, w_in, lq1, lk1, lq2, lk2,
                subln, tq, tk):
    bsz, t, d = x.shape
    flat = x.reshape(bsz * t, d)
    proj = lambda lo, hi: norm_proj(flat, norm_w, w_in[:, lo:hi].astype(BF16), hi - lo).reshape(bsz, t, hi - lo)
    q = proj(0, MIX_W)
    k = proj(MIX_W, 2 * MIX_W)
    v = proj(2 * MIX_W, 3 * MIX_W)
    w_gq = jnp.concatenate([w_in[:, 3 * MIX_W + MEM_W:], w_in[:, 3 * MIX_W:3 * MIX_W + MEM_W]], axis=1).astype(BF16)
    gq = norm_proj(flat, norm_w, w_gq, (BRANCH + MEM_W) // 2).reshape(bsz, t, BRANCH + MEM_W)
    if k_past is None:
        k_all, v_all = k, v
    else:
        past = k_past.shape[1]
        k_all = jnp.concatenate([k_past.reshape(bsz, past, MIX_W), k], axis=1)
        v_all = jnp.concatenate([v_past.reshape(bsz, past, MIX_W), v], axis=1)
    lam_init = 0.8 - 0.6 * math.exp(-0.3 * layer_idx)
    o = diff_attn(q, k_all, v_all, lq1, lk1, lq2, lk2, subln, lam_init, tq, tk)
    y = gate_out(x, o, gq, mk, mv, w_out.astype(BF16), final_w, final=True)
    return y, k.reshape(bsz, t, DF_HEADS, 2 * DF_HD), v.reshape(bsz, t, DF_HEADS, DF_VD)


def kernel(x_prompt, mem_prompt, x_sample, state_rwkv, state_shift, cache_k, cache_v, cache_mem_k, cache_mem_v, norm_w, mem_norm_w, w_mem_k, w_mem_v, w_out, final_norm_w, rw_in, rw_mu, rw_w0, rw_w_up, rw_a0, rw_a_up, rw_k_k, rw_k_a, rw_r_k, rw_ln_w, rw_ln_b, df_in, df_lq1, df_lk1, df_lq2, df_lk2, df_subln):
    bp, tp, _ = x_prompt.shape
    bs, tsm, _ = x_sample.shape
    mem_s = lambda m: m.reshape(bs, N_MEM, MEM_W)
    mk0, mv0 = _mem_kv(mem_prompt, mem_norm_w[0], w_mem_k[0], w_mem_v[0])
    mk1, mv1 = _mem_kv(mem_prompt, mem_norm_w[1], w_mem_k[1], w_mem_v[1])

    rw = (rw_in[0], rw_mu[0], rw_w0[0], rw_w_up[0], rw_a0[0], rw_a_up[0], rw_k_k[0], rw_k_a[0], rw_r_k[0],
          rw_ln_w[0], rw_ln_b[0])
    xp, p_s, p_shift = _rwkv_layer(
        x_prompt, mk0, mv0, jnp.zeros((bp, RW_PROJ), F32), jnp.zeros((bp, RW_HEADS, RW_HD, RW_HD), F32),
        norm_w[0], w_out[0], final_norm_w, *rw, ts=256, cl=64)
    xs, s_s, s_shift = _rwkv_layer(
        x_sample, mem_s(cache_mem_k[0]), mem_s(cache_mem_v[0]), state_shift[0], state_rwkv[0],
        norm_w[0], w_out[0], final_norm_w, *rw, ts=tsm, cl=tsm)

    df = (df_in[0], df_lq1[0], df_lk1[0], df_lq2[0], df_lk2[0], df_subln[0])
    y_prompt, p_k, p_v = _diff_layer(xp, mk1, mv1, None, None, 1, norm_w[1], w_out[1], final_norm_w, *df,
                                     tq=512, tk=512)
    t_kv = cache_k.shape[2] + tsm
    y_sample, s_k, s_v = _diff_layer(xs, mem_s(cache_mem_k[1]), mem_s(cache_mem_v[1]), cache_k[0], cache_v[0], 1,
                                     norm_w[1], w_out[1], final_norm_w, *df, tq=tsm, tk=t_kv)

    mem4 = lambda m: m.reshape(bp, N_MEM, MEM_HEADS, MEM_HD)
    return (y_prompt, y_sample, p_s[None], p_shift[None], p_k[None], p_v[None],
            jnp.stack([mem4(mk0), mem4(mk1)]), jnp.stack([mem4(mv0), mem4(mv1)]),
            s_s[None], s_shift[None], s_k[None], s_v[None])
```

```python
import functools
import math

import jax
import jax.numpy as jnp
from jax import lax
from jax.experimental import pallas as pl
from jax.experimental.pallas import tpu as pltpu

D_MODEL = 1024
CHUNK = 64
CHUNK_SHIFT = 6
N_MEM = 256
BRANCH = 2 * D_MODEL
MIX_W = 3 * D_MODEL // 2
MEM_W = D_MODEL // 2
MEM_HEADS = 4
MEM_HD = MEM_W // MEM_HEADS
RW_HD = 64
RW_HEADS = MIX_W // RW_HD
LORA = 64
RW_PROJ = 3 * MIX_W + 2 * LORA
DF_HD = 64
DF_HEADS = MIX_W // (2 * DF_HD)
DF_VD = 2 * DF_HD
EPS = 1e-6
GN_EPS = 64e-5
NEG_INF = -1e30
LOG2_E = 1.4426950408889634

LANES = 128
RW_PAIRS = RW_HEADS // 2
RW_PROJ_PAD = 4864
V7X_VMEM_LIMIT = 56 * 1024 * 1024

BF16 = jnp.bfloat16
F32 = jnp.float32


def _mm(a, b):
    return jnp.dot(a.astype(BF16), b.astype(BF16), preferred_element_type=F32)


def _mm_nt(a, b):
    return lax.dot_general(a.astype(BF16), b.astype(BF16), (((1,), (1,)), ((), ())),
                           preferred_element_type=F32)


def _mm_tn(a, b):
    return lax.dot_general(a.astype(BF16), b.astype(BF16), (((0,), (0,)), ((), ())),
                           preferred_element_type=F32)


def _sigmoid(x):
    return 1.0 / (1.0 + jnp.exp(-x))


def _softplus(x):
    return jnp.maximum(x, 0.0) + jnp.log(1.0 + jnp.exp(-jnp.abs(x)))


def _params(sem):
    return pltpu.CompilerParams(dimension_semantics=sem, vmem_limit_bytes=V7X_VMEM_LIMIT)


def _norm_proj_kernel(x_ref, nw_ref, w_ref, o_ref, h_ref):
    @pl.when(pl.program_id(1) == 0)
    def _():
        x = x_ref[...]
        ms = jnp.mean(x * x, axis=-1, keepdims=True)
        h_ref[...] = (x * lax.rsqrt(ms + EPS) * nw_ref[...]).astype(BF16)

    o_ref[...] = jnp.dot(h_ref[...], w_ref[...], preferred_element_type=F32)


def norm_proj(x, norm_w, w, bn):
    n, d = x.shape
    nout = w.shape[1]
    bm = min(512, n)
    assert n % bm == 0 and nout % bn == 0
    return pl.pallas_call(
        _norm_proj_kernel,
        grid=(n // bm, nout // bn),
        in_specs=[pl.BlockSpec((bm, d), lambda i, j: (i, 0)),
                  pl.BlockSpec((1, d), lambda i, j: (0, 0)),
                  pl.BlockSpec((d, bn), lambda i, j: (0, j))],
        out_specs=pl.BlockSpec((bm, bn), lambda i, j: (i, j)),
        out_shape=jax.ShapeDtypeStruct((n, nout), F32),
        scratch_shapes=[pltpu.VMEM((bm, d), BF16)],
        compiler_params=_params(("parallel", "arbitrary")),
        name="norm_proj",
    )(x, norm_w.reshape(1, d), w)


def _split3(x):
    hi = x.astype(BF16)
    r1 = x - hi.astype(F32)
    mid = r1.astype(BF16)
    lo = (r1 - mid.astype(F32)).astype(BF16)
    return hi, mid, lo


def _rwkv_kernel(r_ref, k_ref, v_ref, pp_ref, prev_ref, mu_ref, w0_ref, a0_ref, kk_ref, ka_ref, rk_ref,
                 lnw_ref, lnb_ref, wup_ref, aup_ref, z0_ref,
                 y_ref, zout_ref,
                 z_scr, carry, *, ts, cl):
    nc = ts // cl
    rb = 2 * cl
    c_idx = pl.program_id(1)
    k_off, v_off, pp_off = MIX_W, 2 * MIX_W, 3 * MIX_W

    @pl.when(c_idx == 0)
    def _init():
        z_scr[...] = z0_ref[0]
        carry[...] = prev_ref[0]

    def shifted(x, lo):
        hi = lo + x.shape[1]
        row = lax.broadcasted_iota(jnp.int32, x.shape, 0)
        prev = jnp.where(row == 0, carry[:, lo:hi], pltpu.roll(x, 1, 0))
        carry[:, lo:hi] = x[ts - 1:ts, :]
        return x + mu_ref[:, lo:hi] * (prev - x)

    r = shifted(r_ref[0], 0)
    k = shifted(k_ref[0], k_off)
    v = shifted(v_ref[0], v_off)
    pp = shifted(pp_ref[0], pp_off)

    lw = w0_ref[...] + _mm(jnp.tanh(pp), wup_ref[...])
    log_w = -_softplus(-lw) - 0.5
    d = -jnp.exp(log_w)
    a = _sigmoid(a0_ref[...] + _mm(pp, aup_ref[...]))

    li = lax.broadcasted_iota(jnp.int32, (LANES, LANES), 0)
    lj = lax.broadcasted_iota(jnp.int32, (LANES, LANES), 1)
    seg = jnp.where((li >> 6) == (lj >> 6), 1.0, 0.0).astype(BF16)
    pair_lanes = [slice(g * LANES, (g + 1) * LANES) for g in range(RW_PAIRS)]

    def head_sum(x):
        return jnp.concatenate([_mm(x[:, lg], seg) for lg in pair_lanes], axis=1)

    kk = k * kk_ref[...]
    kk = kk / jnp.maximum(jnp.sqrt(head_sum(kk * kk)), 1e-12)
    k2 = k * (1.0 + (a - 1.0) * ka_ref[...])
    b = kk * a
    bonus = head_sum(r * k2 * rk_ref[...]) * v

    ci = lax.broadcasted_iota(jnp.int32, (cl, cl), 0)
    cj = lax.broadcasted_iota(jnp.int32, (cl, cl), 1)
    tril = jnp.where(ci >= cj, 1.0, 0.0).astype(BF16)
    rows = [slice(c0 * cl, (c0 + 1) * cl) for c0 in range(nc)]
    parts = _split3(d)
    cs_c = [sum(jnp.dot(tril, part[sl], preferred_element_type=F32) for part in parts) for sl in rows]
    pc_c = [jnp.exp(x[cl - 1:cl, :]) for x in cs_c]
    cat_rows = lambda xs: xs[0] if nc == 1 else jnp.concatenate(xs, axis=0)
    cs = cat_rows(cs_c)
    pc = cat_rows([jnp.broadcast_to(x, (cl, MIX_W)) for x in pc_c])
    e_nc = jnp.exp(-cs)
    at = -kk * jnp.exp(cs - d)
    rt = r * jnp.exp(cs)
    bt = b * e_nc
    kt = k2 * e_nc
    bh = bt * pc
    kh = kt * pc

    lane_c = lax.broadcasted_iota(jnp.int32, (cl, LANES), 1)
    first = lane_c < RW_HD
    si = lax.broadcasted_iota(jnp.int32, (rb, rb), 0)
    sj = lax.broadcasted_iota(jnp.int32, (rb, rb), 1)
    strict = si > sj
    incl = si >= sj
    eye = jnp.where(si == sj, 1.0, 0.0)

    def level_mask(sh):
        return (((si >> (sh + 1)) == (sj >> (sh + 1)))
                & (((si >> sh) & 1) == 1) & (((sj >> sh) & 1) == 0))

    n_levels = int(math.log2(cl))

    chains = [(sl, lg) for sl in rows for lg in pair_lanes]
    each = lambda fn, *lists: [fn(*xs) for xs in zip(*lists)]

    def stacked(x):
        def one(sl, lg):
            t = x[sl, lg]
            return jnp.concatenate([jnp.where(first, t, 0.0), jnp.where(first, 0.0, t)], axis=0)
        return [one(sl, lg) for sl, lg in chains]

    ats, rts, bts, kts, bhs, khs, vs = map(stacked, (at, rt, bt, kt, bh, kh, v))
    if rb == LANES:
        g = each(lambda a_, r_, b_, k_: _mm_nt(jnp.concatenate([a_, r_], axis=0),
                                               jnp.concatenate([b_, k_], axis=0)), ats, rts, bts, kts)
        g_ab, g_ak = [x[:rb, :rb] for x in g], [x[:rb, rb:] for x in g]
        g_rb, g_rk = [x[rb:, :rb] for x in g], [x[rb:, rb:] for x in g]
    else:
        g_ab, g_ak = each(_mm_nt, ats, bts), each(_mm_nt, ats, kts)
        g_rb, g_rk = each(_mm_nt, rts, bts), each(_mm_nt, rts, kts)
    a_ab = each(lambda x: jnp.where(strict, x, 0.0), g_ab)
    a_ak = each(lambda x: jnp.where(strict, x, 0.0), g_ak)
    a_rb = each(lambda x: jnp.where(incl, x, 0.0), g_rb)
    a_rk = each(lambda x: jnp.where(incl, x, 0.0), g_rk)
    t = each(lambda x: eye + jnp.where(level_mask(0), x, 0.0), a_ab)
    for sh in range(1, n_levels):
        lm = level_mask(sh)
        ta = each(lambda t_, x: _mm(t_, jnp.where(lm, x, 0.0)), t, a_ab)
        t = each(lambda t_, x: t_ + _mm(x, t_), t, ta)
    w = each(_mm, a_ak, vs)
    tw = each(lambda t_, a_, w_: _mm(t_, jnp.concatenate([a_, w_], axis=1)), t, ats, w)
    ap, vp = [x[:, :LANES] for x in tw], [x[:, LANES:] for x in tw]
    rp = each(lambda r_, a_, x: r_ + _mm(a_, x), rts, a_rb, ap)
    if rb == LANES:
        yp = each(lambda a_, c_, x, y_: _mm(jnp.concatenate([a_, c_], axis=1),
                                            jnp.concatenate([x, y_], axis=0)), a_rb, a_rk, vp, vs)
    else:
        yp = each(lambda a_, c_, x, y_: _mm(a_, x) + _mm(c_, y_), a_rb, a_rk, vp, vs)
    mn = each(_mm_tn, bhs, tw)
    diag = li == lj
    pcs = [x[:, lg] for x in pc_c for lg in pair_lanes]
    mm = each(lambda x, p: jnp.where(diag, jnp.broadcast_to(p, (LANES, LANES)), 0.0) + x[:, :LANES], mn, pcs)
    nn = each(lambda x, k_, y_: x[:, LANES:] + _mm_tn(k_, y_), mn, khs, vs)

    z = [z_scr[gi] for gi in range(RW_PAIRS)]
    y_rows = []
    for c0 in range(nc):
        base = c0 * RW_PAIRS
        y_st = [_mm(rp[base + gi], z[gi]) + yp[base + gi] for gi in range(RW_PAIRS)]
        z = [_mm(mm[base + gi], z[gi]) + nn[base + gi] for gi in range(RW_PAIRS)]
        y_rows.append(jnp.concatenate([x[:cl] + x[cl:] for x in y_st], axis=1))
    for gi in range(RW_PAIRS):
        z_scr[gi] = z[gi]
    y = cat_rows(y_rows)

    inv_hd = 1.0 / RW_HD
    yc = y - head_sum(y) * inv_hd
    var = head_sum(yc * yc) * inv_hd
    y_ref[0] = yc * lax.rsqrt(var + GN_EPS) * lnw_ref[...] + lnb_ref[...] + bonus

    @pl.when(c_idx == pl.num_programs(1) - 1)
    def _fin():
        for gi in range(RW_PAIRS):
            zout_ref[0, gi] = z[gi]


def rwkv_core(p, prev_row, z0, mu, w0, a0, k_k, k_a, r_k, ln_w, ln_b, wup, aup, ts, cl):
    bsz, t, _ = p.shape
    assert t % ts == 0 and ts % cl == 0
    tok = lambda off: pl.BlockSpec((1, ts, MIX_W), lambda b, c, off=off: (b, c, off))
    chan = pl.BlockSpec((1, MIX_W), lambda b, c: (0, 0))
    lora = pl.BlockSpec((LANES, MIX_W), lambda b, c: (0, 0))
    state = pl.BlockSpec((1, RW_PAIRS, LANES, LANES), lambda b, c: (b, 0, 0, 0))
    in_specs = [tok(0), tok(1), tok(2),
                pl.BlockSpec((1, ts, LANES), lambda b, c: (b, c, 3 * MIX_W // LANES)),
                pl.BlockSpec((1, 1, RW_PROJ), lambda b, c: (b, 0, 0)),
                pl.BlockSpec((1, RW_PROJ), lambda b, c: (0, 0)),
                chan, chan, chan, chan, chan, chan, chan, lora, lora, state]
    row = lambda a: a.reshape(1, -1)
    return pl.pallas_call(
        functools.partial(_rwkv_kernel, ts=ts, cl=cl),
        grid=(bsz, t // ts),
        in_specs=in_specs,
        out_specs=[pl.BlockSpec((1, ts, MIX_W), lambda b, c: (b, c, 0)), state],
        out_shape=[jax.ShapeDtypeStruct((bsz, t, MIX_W), F32),
                   jax.ShapeDtypeStruct((bsz, RW_PAIRS, LANES, LANES), F32)],
        scratch_shapes=[pltpu.VMEM((RW_PAIRS, LANES, LANES), F32), pltpu.VMEM((1, RW_PROJ), F32)],
        compiler_params=_params(("parallel", "arbitrary")),
        name="rwkv_core",
    )(p, p, p, p, prev_row, row(mu), row(w0), row(a0), row(k_k), row(k_a), row(r_k), row(ln_w), row(ln_b),
      wup, aup, z0)


def _pair_state_in(s):
    bsz = s.shape[0]
    st = jnp.swapaxes(s.astype(F32), -1, -2).reshape(bsz, RW_PAIRS, 2, RW_HD, RW_HD)
    z = st[:, :, :, :, None, :] * jnp.eye(2, dtype=F32)[None, None, :, None, :, None]
    return z.reshape(bsz, RW_PAIRS, LANES, LANES)


def _pair_state_out(z):
    bsz = z.shape[0]
    z6 = z.reshape(bsz, RW_PAIRS, 2, RW_HD, 2, RW_HD)
    st = jnp.stack([z6[:, :, 0, :, 0, :], z6[:, :, 1, :, 1, :]], axis=2)
    return jnp.swapaxes(st.reshape(bsz, RW_HEADS, RW_HD, RW_HD), -1, -2)


def _diff_attn_kernel(q_ref, k_ref, v_ref, lq1, lk1, lq2, lk2, sub_ref, o_ref, kb, vbt,
                      *, tq, tk, t_kv, t_valid, past, lam_init):
    qi = pl.program_id(2)

    @pl.when(qi == 0)
    def _stage():
        kb[...] = k_ref[0].astype(BF16)

        def body(c, _):
            start = pl.multiple_of(c * LANES, LANES)
            vbt[:, pl.ds(start, LANES)] = v_ref[0, pl.ds(start, LANES), :].T.astype(BF16)
            return 0

        lax.fori_loop(0, t_kv // LANES, body, 0)

    lam = (jnp.exp(jnp.sum(lq1[...] * lk1[...], axis=-1, keepdims=True))
           - jnp.exp(jnp.sum(lq2[...] * lk2[...], axis=-1, keepdims=True)) + lam_init)

    lane = lax.broadcasted_iota(jnp.int32, (tq, LANES), 1)
    q = q_ref[0] * (DF_HD ** -0.5 * LOG2_E)
    q1 = jnp.where(lane < DF_HD, q, 0.0).astype(BF16)
    q2 = jnp.where(lane >= DF_HD, q, 0.0).astype(BF16)

    q0 = past + qi * tq
    lim_first = jnp.minimum(((q0 >> CHUNK_SHIFT) + 1) << CHUNK_SHIFT, t_valid)
    lim_last = jnp.minimum((((q0 + tq - 1) >> CHUNK_SHIFT) + 1) << CHUNK_SHIFT, t_valid)
    n_full = lim_first // tk
    n_tot = (lim_last + tk - 1) // tk

    tw = min(tq, 2 * LANES)
    n_blk = tq // tw
    q_str = [qh[i * tw:(i + 1) * tw] for qh in (q1, q2) for i in range(n_blk)]
    q_pos = q0 + lax.broadcasted_iota(jnp.int32, (1, tq), 1)
    q_lim = jnp.minimum(((q_pos >> CHUNK_SHIFT) + 1) << CHUNK_SHIFT, t_valid)
    lim_str = [q_lim[:, i * tw:(i + 1) * tw] for _ in range(2) for i in range(n_blk)]
    k_idx = lax.broadcasted_iota(jnp.int32, (tk, tw), 0)

    def step(j, carry, masked):
        start = pl.multiple_of(j * tk, tk)
        kc = kb[pl.ds(start, tk), :]
        vct = vbt[:, pl.ds(start, tk)]
        m_old, l_old, acc_old = carry
        s = [lax.dot_general(kc, qs, (((1,), (1,)), ((), ())), preferred_element_type=F32) for qs in q_str]
        if masked:
            s = [jnp.where(k_idx + start < lim, x, NEG_INF) for x, lim in zip(s, lim_str)]
        m_new = [jnp.maximum(m, jnp.max(x, axis=0, keepdims=True)) for m, x in zip(m_old, s)]
        corr = [jnp.exp2(m - mn) for m, mn in zip(m_old, m_new)]
        p = [jnp.exp2(x - mn) for x, mn in zip(s, m_new)]
        l_new = [l * c + jnp.sum(x, axis=0, keepdims=True) for l, c, x in zip(l_old, corr, p)]
        acc_new = [a * c + jnp.dot(vct, x.astype(BF16), preferred_element_type=F32)
                   for a, c, x in zip(acc_old, corr, p)]
        return tuple(m_new), tuple(l_new), tuple(acc_new)

    n_str = 2 * n_blk
    carry = (tuple(jnp.full((1, tw), NEG_INF, F32) for _ in range(n_str)),
             tuple(jnp.zeros((1, tw), F32) for _ in range(n_str)),
             tuple(jnp.zeros((LANES, tw), F32) for _ in range(n_str)))
    carry = lax.fori_loop(0, n_full, functools.partial(step, masked=False), carry)
    carry = lax.fori_loop(n_full, n_tot, functools.partial(step, masked=True), carry)
    _, l_fin, acc_fin = carry
    cat = lambda xs: xs[0] if len(xs) == 1 else jnp.concatenate(xs, axis=1)
    l1, l2 = cat(l_fin[:n_blk]), cat(l_fin[n_blk:])
    acc1, acc2 = cat(acc_fin[:n_blk]), cat(acc_fin[n_blk:])
    o_t = acc1 / l1 - lam * (acc2 / l2)
    ms = jnp.mean(o_t * o_t, axis=0, keepdims=True)
    o_t = o_t * lax.rsqrt(ms + EPS)
    o_ref[0] = o_t.T * sub_ref[...] * (1.0 - lam_init)


def diff_attn(q, k_all, v_all, past, t_valid, lq1, lk1, lq2, lk2, subln, lam_init, tq, tk):
    bsz, t_q, _ = q.shape
    t_kv = k_all.shape[1]
    assert t_q % tq == 0 and t_kv % tk == 0 and t_kv % LANES == 0 and tq % LANES == 0
    kern = functools.partial(_diff_attn_kernel, tq=tq, tk=tk, t_kv=t_kv, t_valid=t_valid,
                             past=past, lam_init=lam_init)
    vec = pl.BlockSpec((1, DF_HD), lambda b, h, i: (0, 0))
    kv = pl.BlockSpec((1, t_kv, LANES), lambda b, h, i: (b, 0, h))
    row = lambda a: a.reshape(1, -1)
    return pl.pallas_call(
        kern,
        grid=(bsz, DF_HEADS, t_q // tq),
        in_specs=[pl.BlockSpec((1, tq, LANES), lambda b, h, i: (b, i, h)), kv, kv,
                  vec, vec, vec, vec, pl.BlockSpec((1, DF_VD), lambda b, h, i: (0, 0))],
        out_specs=pl.BlockSpec((1, tq, LANES), lambda b, h, i: (b, i, h)),
        out_shape=jax.ShapeDtypeStruct((bsz, t_q, MIX_W), F32),
        scratch_shapes=[pltpu.VMEM((t_kv, LANES), BF16), pltpu.VMEM((LANES, t_kv), BF16)],
        compiler_params=_params(("parallel", "parallel", "arbitrary")),
        name="diff_attn",
    )(q, k_all, v_all, row(lq1), row(lk1), row(lq2), row(lk2), row(subln))


def _gate_out_kernel(x_ref, ymix_ref, gq_ref, mk_ref, mv_ref, wout_ref, fw_ref, o_ref, *, final):
    gq = gq_ref[0]
    gate = gq[:, :BRANCH]
    heads = []
    for h in range(MEM_HEADS):
        sl = slice(h * MEM_HD, (h + 1) * MEM_HD)
        qh = gq[:, BRANCH + h * MEM_HD:BRANCH + (h + 1) * MEM_HD]
        s = _mm_nt(qh, mk_ref[0][:, sl]) * (MEM_HD ** -0.5)
        e = jnp.exp(s - jnp.max(s, axis=-1, keepdims=True))
        prob = e / jnp.sum(e, axis=-1, keepdims=True)
        heads.append(_mm(prob, mv_ref[0][:, sl]))
    branch = jnp.concatenate([ymix_ref[0]] + heads, axis=-1)
    act = (branch * (gate * _sigmoid(gate))).astype(BF16)
    xn = x_ref[0] + jnp.dot(act, wout_ref[...], preferred_element_type=F32)
    if final:
        ms = jnp.mean(xn * xn, axis=-1, keepdims=True)
        xn = xn * lax.rsqrt(ms + EPS) * fw_ref[...]
    o_ref[0] = xn


def gate_out(x, ymix, gq, mk, mv, w_out, final_w, final):
    bsz, t, d = x.shape
    bm = min(512, t)
    assert t % bm == 0
    tokb = lambda w: pl.BlockSpec((1, bm, w), lambda b, i: (b, i, 0))
    mem = pl.BlockSpec((1, N_MEM, MEM_W), lambda b, i: (b, 0, 0))
    return pl.pallas_call(
        functools.partial(_gate_out_kernel, final=final),
        grid=(bsz, t // bm),
        in_specs=[tokb(d), tokb(MIX_W), tokb(BRANCH + MEM_W), mem, mem,
                  pl.BlockSpec((BRANCH, d), lambda b, i: (0, 0)),
                  pl.BlockSpec((1, d), lambda b, i: (0, 0))],
        out_specs=tokb(d),
        out_shape=jax.ShapeDtypeStruct((bsz, t, d), F32),
        compiler_params=_params(("parallel", "arbitrary")),
        name="gate_out",
    )(x, ymix, gq, mk, mv, w_out, final_w.reshape(1, d))


def _mem_kv(mem, norm_w, wk, wv):
    bsz = mem.shape[0]
    flat = mem.reshape(bsz * N_MEM, D_MODEL)
    mk = norm_proj(flat, norm_w, wk.astype(BF16), MEM_W)
    mv = norm_proj(flat, norm_w, wv.astype(BF16), MEM_W)
    return mk.reshape(bsz, N_MEM, MEM_W), mv.reshape(bsz, N_MEM, MEM_W)


def _rwkv_layer(x, mk, mv, prev_row, s0, norm_w, w_out, final_w, w_in, mu, w0, w_up, a0, a_up,
                k_k, k_a, r_k, ln_w, ln_b, ts, cl):
    bsz, t, d = x.shape
    flat = x.reshape(bsz * t, d)
    w_p = jnp.pad(w_in[:, :RW_PROJ], ((0, 0), (0, RW_PROJ_PAD - RW_PROJ))).astype(BF16)
    w_gq = jnp.concatenate([w_in[:, RW_PROJ + MEM_W:], w_in[:, RW_PROJ:RW_PROJ + MEM_W]], axis=1).astype(BF16)
    p = norm_proj(flat, norm_w, w_p, RW_PROJ_PAD // 2).reshape(bsz, t, RW_PROJ_PAD)
    gq = norm_proj(flat, norm_w, w_gq, (BRANCH + MEM_W) // 2).reshape(bsz, t, BRANCH + MEM_W)
    zero = jnp.zeros((LORA, MIX_W), F32)
    wup = jnp.concatenate([w_up, zero], axis=0).astype(BF16)
    aup = jnp.concatenate([zero, a_up], axis=0).astype(BF16)
    y_mix, z_t = rwkv_core(p, prev_row.reshape(bsz, 1, RW_PROJ), _pair_state_in(s0), mu, w0, a0, k_k, k_a,
                           r_k.reshape(-1), ln_w, ln_b, wup, aup, ts, cl)
    x_new = gate_out(x, y_mix, gq, mk, mv, w_out.astype(BF16), final_w, final=False)
    return x_new, _pair_state_out(z_t), p[:, t - 1, :RW_PROJ]


def _round_up(n, m):
    return (n + m - 1) // m * m


def _diff_layer(x, mk, mv, k_past, v_past, layer_idx, norm_w, w_out, final_w, w_in, lq1, lk1, lq2, lk2,
                subln, tq, tk):
    bsz, t, d = x.shape
    flat = x.reshape(bsz * t, d)
    proj = lambda lo, hi: norm_proj(flat, norm_w, w_in[:, lo:hi].astype(BF16), hi - lo).reshape(bsz, t, hi - lo)
    q = proj(0, MIX_W)
    k = proj(MIX_W, 2 * MIX_W)
    v = proj(2 * MIX_W, 3 * MIX_W)
    w_gq = jnp.concatenate([w_in[:, 3 * MIX_W + MEM_W:], w_in[:, 3 * MIX_W:3 * MIX_W + MEM_W]], axis=1).astype(BF16)
    gq = norm_proj(flat, norm_w, w_gq, (BRANCH + MEM_W) // 2).reshape(bsz, t, BRANCH + MEM_W)
    past = 0 if k_past is None else k_past.shape[1]
    t_valid = past + t
    t_q, t_kv = _round_up(t, tq), _round_up(t_valid, tk)
    pad_rows = lambda a, n: a if n == 0 else jnp.pad(a, ((0, 0), (0, n), (0, 0)))
    k_all, v_all = k, v
    if k_past is not None:
        k_all = jnp.concatenate([k_past.reshape(bsz, past, MIX_W), k], axis=1)
        v_all = jnp.concatenate([v_past.reshape(bsz, past, MIX_W), v], axis=1)
    lam_init = 0.8 - 0.6 * math.exp(-0.3 * layer_idx)
    o = diff_attn(pad_rows(q, t_q - t), pad_rows(k_all, t_kv - t_valid), pad_rows(v_all, t_kv - t_valid),
                  past, t_valid, lq1, lk1, lq2, lk2, subln, lam_init, tq, tk)[:, :t]
    y = gate_out(x, o, gq, mk, mv, w_out.astype(BF16), final_w, final=True)
    return y, k.reshape(bsz, t, DF_HEADS, 2 * DF_HD), v.reshape(bsz, t, DF_HEADS, DF_VD)


def kernel(x_prompt, mem_prompt, x_sample, state_rwkv, state_shift, cache_k, cache_v, cache_mem_k, cache_mem_v, norm_w, mem_norm_w, w_mem_k, w_mem_v, w_out, final_norm_w, rw_in, rw_mu, rw_w0, rw_w_up, rw_a0, rw_a_up, rw_k_k, rw_k_a, rw_r_k, rw_ln_w, rw_ln_b, df_in, df_lq1, df_lk1, df_lq2, df_lk2, df_subln):
    bp, tp, _ = x_prompt.shape
    bs, tsm, _ = x_sample.shape
    mem_s = lambda m: m.reshape(bs, N_MEM, MEM_W)
    mk0, mv0 = _mem_kv(mem_prompt, mem_norm_w[0], w_mem_k[0], w_mem_v[0])
    mk1, mv1 = _mem_kv(mem_prompt, mem_norm_w[1], w_mem_k[1], w_mem_v[1])

    rw = (rw_in[0], rw_mu[0], rw_w0[0], rw_w_up[0], rw_a0[0], rw_a_up[0], rw_k_k[0], rw_k_a[0], rw_r_k[0],
          rw_ln_w[0], rw_ln_b[0])
    xp, p_s, p_shift = _rwkv_layer(
        x_prompt, mk0, mv0, jnp.zeros((bp, RW_PROJ), F32), jnp.zeros((bp, RW_HEADS, RW_HD, RW_HD), F32),
        norm_w[0], w_out[0], final_norm_w, *rw, ts=64, cl=64)
    xs, s_s, s_shift = _rwkv_layer(
        x_sample, mem_s(cache_mem_k[0]), mem_s(cache_mem_v[0]), state_shift[0], state_rwkv[0],
        norm_w[0], w_out[0], final_norm_w, *rw, ts=tsm, cl=tsm)

    df = (df_in[0], df_lq1[0], df_lk1[0], df_lq2[0], df_lk2[0], df_subln[0])
    y_prompt, p_k, p_v = _diff_layer(xp, mk1, mv1, None, None, 1, norm_w[1], w_out[1], final_norm_w, *df,
                                     tq=512, tk=512)
    y_sample, s_k, s_v = _diff_layer(xs, mem_s(cache_mem_k[1]), mem_s(cache_mem_v[1]), cache_k[0], cache_v[0], 1,
                                     norm_w[1], w_out[1], final_norm_w, *df, tq=128, tk=384)

    mem4 = lambda m: m.reshape(bp, N_MEM, MEM_HEADS, MEM_HD)
    return (y_prompt, y_sample, p_s[None], p_shift[None], p_k[None], p_v[None],
            jnp.stack([mem4(mk0), mem4(mk1)]), jnp.stack([mem4(mv0), mem4(mv1)]),
            s_s[None], s_shift[None], s_k[None], s_v[None])
```

```python
import functools
import math

import jax
import jax.numpy as jnp
from jax import lax
from jax.experimental import pallas as pl
from jax.experimental.pallas import tpu as pltpu

D_MODEL = 1024
CHUNK = 64
CHUNK_SHIFT = 6
N_MEM = 256
BRANCH = 2 * D_MODEL
MIX_W = 3 * D_MODEL // 2
MEM_W = D_MODEL // 2
MEM_HEADS = 4
MEM_HD = MEM_W // MEM_HEADS
RW_HD = 64
RW_HEADS = MIX_W // RW_HD
LORA = 64
RW_PROJ = 3 * MIX_W + 2 * LORA
DF_HD = 64
DF_HEADS = MIX_W // (2 * DF_HD)
DF_VD = 2 * DF_HD
EPS = 1e-6
GN_EPS = 64e-5
NEG_INF = -1e30
LOG2_E = 1.4426950408889634

LANES = 128
RW_PAIRS = RW_HEADS // 2
RW_PROJ_PAD = 4864
V7X_VMEM_LIMIT = 56 * 1024 * 1024

ROW_BLOCK = 1024
GATE_ROWS = 512
RW_CHUNK = 64
ATTN_TILE = 512

BF16 = jnp.bfloat16
F32 = jnp.float32


def _mm(a, b):
    return jnp.dot(a.astype(BF16), b.astype(BF16), preferred_element_type=F32)


def _mm_nt(a, b):
    return lax.dot_general(a.astype(BF16), b.astype(BF16), (((1,), (1,)), ((), ())),
                           preferred_element_type=F32)


def _mm_tn(a, b):
    return lax.dot_general(a.astype(BF16), b.astype(BF16), (((0,), (0,)), ((), ())),
                           preferred_element_type=F32)


def _sigmoid(x):
    return 1.0 / (1.0 + jnp.exp(-x))


def _softplus(x):
    return jnp.maximum(x, 0.0) + jnp.log(1.0 + jnp.exp(-jnp.abs(x)))


def _rms(x):
    return x * lax.rsqrt(jnp.mean(x * x, axis=-1, keepdims=True) + EPS)


def _params(sem):
    return pltpu.CompilerParams(dimension_semantics=sem, vmem_limit_bytes=V7X_VMEM_LIMIT)


def _rmsnorm_kernel(x_ref, nw_ref, o_ref):
    o_ref[...] = (_rms(x_ref[...]) * nw_ref[...]).astype(o_ref.dtype)


def rmsnorm_bf16(x, norm_w):
    n, d = x.shape
    bm = min(ROW_BLOCK, n)
    assert n % bm == 0
    return pl.pallas_call(
        _rmsnorm_kernel,
        grid=(n // bm,),
        in_specs=[pl.BlockSpec((bm, d), lambda i: (i, 0)), pl.BlockSpec((1, d), lambda i: (0, 0))],
        out_specs=pl.BlockSpec((bm, d), lambda i: (i, 0)),
        out_shape=jax.ShapeDtypeStruct((n, d), BF16),
        compiler_params=_params(("parallel",)),
        name="rmsnorm",
    )(x, norm_w.reshape(1, d))


def _proj_kernel(h_ref, w_ref, o_ref):
    o_ref[...] = jnp.dot(h_ref[...], w_ref[...], preferred_element_type=F32).astype(o_ref.dtype)


def proj(h, w, bn, out_dtype):
    n, d = h.shape
    nout = w.shape[1]
    bm = min(ROW_BLOCK, n)
    assert n % bm == 0 and nout % bn == 0
    return pl.pallas_call(
        _proj_kernel,
        grid=(nout // bn, n // bm),
        in_specs=[pl.BlockSpec((bm, d), lambda j, i: (i, 0)),
                  pl.BlockSpec((d, bn), lambda j, i: (0, j))],
        out_specs=pl.BlockSpec((bm, bn), lambda j, i: (i, j)),
        out_shape=jax.ShapeDtypeStruct((n, nout), out_dtype),
        compiler_params=_params(("parallel", "parallel")),
        name="proj",
    )(h, w)


def _split3(x):
    hi = x.astype(BF16)
    r1 = x - hi.astype(F32)
    mid = r1.astype(BF16)
    lo = (r1 - mid.astype(F32)).astype(BF16)
    return hi, mid, lo


def _rwkv_kernel(r_ref, k_ref, v_ref, pp_ref, prev_ref, mu_ref, w0_ref, a0_ref, kk_ref, ka_ref, rk_ref,
                 lnw_ref, lnb_ref, wup_ref, aup_ref, z0_ref,
                 y_ref, zout_ref,
                 z_scr, carry, *, ts, cl):
    nc = ts // cl
    rb = 2 * cl
    c_idx = pl.program_id(1)
    k_off, v_off, pp_off = MIX_W, 2 * MIX_W, 3 * MIX_W

    @pl.when(c_idx == 0)
    def _init():
        z_scr[...] = z0_ref[0]
        carry[...] = prev_ref[0]

    def shifted(x_ref, lo):
        x = x_ref[0].astype(F32)
        hi = lo + x.shape[1]
        row = lax.broadcasted_iota(jnp.int32, x.shape, 0)
        prev = jnp.where(row == 0, carry[:, lo:hi], pltpu.roll(x, 1, 0))
        carry[:, lo:hi] = x[ts - 1:ts, :]
        return x + mu_ref[:, lo:hi] * (prev - x)

    r = shifted(r_ref, 0)
    k = shifted(k_ref, k_off)
    v = shifted(v_ref, v_off)
    pp = shifted(pp_ref, pp_off)

    lw = w0_ref[...] + _mm(jnp.tanh(pp), wup_ref[...])
    log_w = -_softplus(-lw) - 0.5
    d = -jnp.exp(log_w)
    a = _sigmoid(a0_ref[...] + _mm(pp, aup_ref[...]))

    li = lax.broadcasted_iota(jnp.int32, (LANES, LANES), 0)
    lj = lax.broadcasted_iota(jnp.int32, (LANES, LANES), 1)
    seg = jnp.where((li >> 6) == (lj >> 6), 1.0, 0.0).astype(BF16)
    pair_lanes = [slice(g * LANES, (g + 1) * LANES) for g in range(RW_PAIRS)]

    def head_sum(x):
        return jnp.concatenate([_mm(x[:, lg], seg) for lg in pair_lanes], axis=1)

    kk = k * kk_ref[...]
    kk = kk / jnp.maximum(jnp.sqrt(head_sum(kk * kk)), 1e-12)
    k2 = k * (1.0 + (a - 1.0) * ka_ref[...])
    b = kk * a
    bonus = head_sum(r * k2 * rk_ref[...]) * v

    ci = lax.broadcasted_iota(jnp.int32, (cl, cl), 0)
    cj = lax.broadcasted_iota(jnp.int32, (cl, cl), 1)
    tril = jnp.where(ci >= cj, 1.0, 0.0).astype(BF16)
    rows = [slice(c0 * cl, (c0 + 1) * cl) for c0 in range(nc)]
    parts = _split3(d)
    cs_c = [sum(jnp.dot(tril, part[sl], preferred_element_type=F32) for part in parts) for sl in rows]
    pc_c = [jnp.exp(x[cl - 1:cl, :]) for x in cs_c]
    cat_rows = lambda xs: xs[0] if nc == 1 else jnp.concatenate(xs, axis=0)
    cs = cat_rows(cs_c)
    pc = cat_rows([jnp.broadcast_to(x, (cl, MIX_W)) for x in pc_c])
    e_nc = jnp.exp(-cs)
    at = -kk * jnp.exp(cs - d)
    rt = r * jnp.exp(cs)
    bt = b * e_nc
    kt = k2 * e_nc
    bh = bt * pc
    kh = kt * pc

    lane_c = lax.broadcasted_iota(jnp.int32, (cl, LANES), 1)
    first = lane_c < RW_HD
    si = lax.broadcasted_iota(jnp.int32, (rb, rb), 0)
    sj = lax.broadcasted_iota(jnp.int32, (rb, rb), 1)
    strict = si > sj
    incl = si >= sj
    eye = jnp.where(si == sj, 1.0, 0.0)

    def level_mask(sh):
        return (((si >> (sh + 1)) == (sj >> (sh + 1)))
                & (((si >> sh) & 1) == 1) & (((sj >> sh) & 1) == 0))

    n_levels = int(math.log2(cl))

    chains = [(sl, lg) for sl in rows for lg in pair_lanes]
    each = lambda fn, *lists: [fn(*xs) for xs in zip(*lists)]

    def stacked(x):
        def one(sl, lg):
            t = x[sl, lg]
            return jnp.concatenate([jnp.where(first, t, 0.0), jnp.where(first, 0.0, t)], axis=0)
        return [one(sl, lg) for sl, lg in chains]

    ats, rts, bts, kts, bhs, khs, vs = map(stacked, (at, rt, bt, kt, bh, kh, v))
    if rb == LANES:
        g = each(lambda a_, r_, b_, k_: _mm_nt(jnp.concatenate([a_, r_], axis=0),
                                               jnp.concatenate([b_, k_], axis=0)), ats, rts, bts, kts)
        g_ab, g_ak = [x[:rb, :rb] for x in g], [x[:rb, rb:] for x in g]
        g_rb, g_rk = [x[rb:, :rb] for x in g], [x[rb:, rb:] for x in g]
    else:
        g_ab, g_ak = each(_mm_nt, ats, bts), each(_mm_nt, ats, kts)
        g_rb, g_rk = each(_mm_nt, rts, bts), each(_mm_nt, rts, kts)
    a_ab = each(lambda x: jnp.where(strict, x, 0.0), g_ab)
    a_ak = each(lambda x: jnp.where(strict, x, 0.0), g_ak)
    a_rb = each(lambda x: jnp.where(incl, x, 0.0), g_rb)
    a_rk = each(lambda x: jnp.where(incl, x, 0.0), g_rk)
    t = each(lambda x: eye + jnp.where(level_mask(0), x, 0.0), a_ab)
    for sh in range(1, n_levels):
        lm = level_mask(sh)
        ta = each(lambda t_, x: _mm(t_, jnp.where(lm, x, 0.0)), t, a_ab)
        t = each(lambda t_, x: t_ + _mm(x, t_), t, ta)
    w = each(_mm, a_ak, vs)
    tw = each(lambda t_, a_, w_: _mm(t_, jnp.concatenate([a_, w_], axis=1)), t, ats, w)
    ap, vp = [x[:, :LANES] for x in tw], [x[:, LANES:] for x in tw]
    rp = each(lambda r_, a_, x: r_ + _mm(a_, x), rts, a_rb, ap)
    if rb == LANES:
        yp = each(lambda a_, c_, x, y_: _mm(jnp.concatenate([a_, c_], axis=1),
                                            jnp.concatenate([x, y_], axis=0)), a_rb, a_rk, vp, vs)
    else:
        yp = each(lambda a_, c_, x, y_: _mm(a_, x) + _mm(c_, y_), a_rb, a_rk, vp, vs)
    mn = each(_mm_tn, bhs, tw)
    diag = li == lj
    pcs = [x[:, lg] for x in pc_c for lg in pair_lanes]
    mm = each(lambda x, p: jnp.where(diag, jnp.broadcast_to(p, (LANES, LANES)), 0.0) + x[:, :LANES], mn, pcs)
    nn = each(lambda x, k_, y_: x[:, LANES:] + _mm_tn(k_, y_), mn, khs, vs)

    z = [z_scr[gi] for gi in range(RW_PAIRS)]
    y_rows = []
    for c0 in range(nc):
        base = c0 * RW_PAIRS
        y_st = [_mm(rp[base + gi], z[gi]) + yp[base + gi] for gi in range(RW_PAIRS)]
        z = [_mm(mm[base + gi], z[gi]) + nn[base + gi] for gi in range(RW_PAIRS)]
        y_rows.append(jnp.concatenate([x[:cl] + x[cl:] for x in y_st], axis=1))
    for gi in range(RW_PAIRS):
        z_scr[gi] = z[gi]
    y = cat_rows(y_rows)

    inv_hd = 1.0 / RW_HD
    yc = y - head_sum(y) * inv_hd
    var = head_sum(yc * yc) * inv_hd
    y_ref[0] = (yc * lax.rsqrt(var + GN_EPS) * lnw_ref[...] + lnb_ref[...] + bonus).astype(y_ref.dtype)

    @pl.when(c_idx == pl.num_programs(1) - 1)
    def _fin():
        for gi in range(RW_PAIRS):
            zout_ref[0, gi] = z[gi]


def rwkv_core(p, prev_row, z0, mu, w0, a0, k_k, k_a, r_k, ln_w, ln_b, wup, aup, ts, cl):
    bsz, t, _ = p.shape
    assert t % ts == 0 and ts % cl == 0
    tok = lambda off: pl.BlockSpec((1, ts, MIX_W), lambda b, c, off=off: (b, c, off))
    chan = pl.BlockSpec((1, MIX_W), lambda b, c: (0, 0))
    lora = pl.BlockSpec((LANES, MIX_W), lambda b, c: (0, 0))
    state = pl.BlockSpec((1, RW_PAIRS, LANES, LANES), lambda b, c: (b, 0, 0, 0))
    in_specs = [tok(0), tok(1), tok(2),
                pl.BlockSpec((1, ts, LANES), lambda b, c: (b, c, 3 * MIX_W // LANES)),
                pl.BlockSpec((1, 1, RW_PROJ), lambda b, c: (b, 0, 0)),
                pl.BlockSpec((1, RW_PROJ), lambda b, c: (0, 0)),
                chan, chan, chan, chan, chan, chan, chan, lora, lora, state]
    row = lambda a: a.reshape(1, -1)
    return pl.pallas_call(
        functools.partial(_rwkv_kernel, ts=ts, cl=cl),
        grid=(bsz, t // ts),
        in_specs=in_specs,
        out_specs=[pl.BlockSpec((1, ts, MIX_W), lambda b, c: (b, c, 0)), state],
        out_shape=[jax.ShapeDtypeStruct((bsz, t, MIX_W), BF16),
                   jax.ShapeDtypeStruct((bsz, RW_PAIRS, LANES, LANES), F32)],
        scratch_shapes=[pltpu.VMEM((RW_PAIRS, LANES, LANES), F32), pltpu.VMEM((1, RW_PROJ), F32)],
        compiler_params=_params(("parallel", "arbitrary")),
        name="rwkv_core",
    )(p, p, p, p, prev_row, row(mu), row(w0), row(a0), row(k_k), row(k_a), row(r_k), row(ln_w), row(ln_b),
      wup, aup, z0)


def _pair_state_in(s):
    bsz = s.shape[0]
    st = jnp.swapaxes(s.astype(F32), -1, -2).reshape(bsz, RW_PAIRS, 2, RW_HD, RW_HD)
    z = st[:, :, :, :, None, :] * jnp.eye(2, dtype=F32)[None, None, :, None, :, None]
    return z.reshape(bsz, RW_PAIRS, LANES, LANES)


def _pair_state_out(z):
    bsz = z.shape[0]
    z6 = z.reshape(bsz, RW_PAIRS, 2, RW_HD, 2, RW_HD)
    st = jnp.stack([z6[:, :, 0, :, 0, :], z6[:, :, 1, :, 1, :]], axis=2)
    return jnp.swapaxes(st.reshape(bsz, RW_HEADS, RW_HD, RW_HD), -1, -2)


def _attn_stream_width(tq):
    return min(tq, 2 * LANES)


def _diff_attn_kernel(q_ref, k_ref, v_ref, lq1, lk1, lq2, lk2, sub_ref, o_ref, kb, vbt,
                      *, tq, tk, t_kv, t_valid, past, lam_init):
    qi = pl.program_id(2)

    @pl.when(qi == 0)
    def _stage():
        kb[...] = k_ref[0].astype(BF16)

        def body(c, _):
            start = pl.multiple_of(c * LANES, LANES)
            vbt[:, pl.ds(start, LANES)] = v_ref[0, pl.ds(start, LANES), :].T.astype(BF16)
            return 0

        lax.fori_loop(0, t_kv // LANES, body, 0)

    lam = (jnp.exp(jnp.sum(lq1[...] * lk1[...], axis=-1, keepdims=True))
           - jnp.exp(jnp.sum(lq2[...] * lk2[...], axis=-1, keepdims=True)) + lam_init)

    lane = lax.broadcasted_iota(jnp.int32, (tq, LANES), 1)
    q = q_ref[0].astype(F32) * (DF_HD ** -0.5 * LOG2_E)
    q1 = jnp.where(lane < DF_HD, q, 0.0).astype(BF16)
    q2 = jnp.where(lane >= DF_HD, q, 0.0).astype(BF16)

    q0 = past + qi * tq
    lim_first = jnp.minimum(((q0 >> CHUNK_SHIFT) + 1) << CHUNK_SHIFT, t_valid)
    lim_last = jnp.minimum((((q0 + tq - 1) >> CHUNK_SHIFT) + 1) << CHUNK_SHIFT, t_valid)
    n_full = lim_first // tk
    n_tot = (lim_last + tk - 1) // tk

    tw = _attn_stream_width(tq)
    n_blk = tq // tw
    n_str = 2 * n_blk
    q_str = [qh[i * tw:(i + 1) * tw] for qh in (q1, q2) for i in range(n_blk)]
    q_pos = q0 + lax.broadcasted_iota(jnp.int32, (1, tq), 1)
    q_lim = jnp.minimum(((q_pos >> CHUNK_SHIFT) + 1) << CHUNK_SHIFT, t_valid)
    lim_str = [q_lim[:, i * tw:(i + 1) * tw] for _ in range(2) for i in range(n_blk)]
    diagonal = past == 0 and tq == tk and t_valid == t_kv
    rows_full = [tk] * n_str
    rows_diag = [(i + 1) * tw for _ in range(2) for i in range(n_blk)] if diagonal else rows_full

    def tile(j, rows):
        start = pl.multiple_of(j * tk, tk)
        uniq = {n: (kb[pl.ds(start, n), :], vbt[:, pl.ds(start, n)]) for n in sorted(set(rows))}
        return start, [uniq[n][0] for n in rows], [uniq[n][1] for n in rows]

    def qk(kcs):
        return [lax.dot_general(kc, qs, (((1,), (1,)), ((), ())), preferred_element_type=F32)
                for kc, qs in zip(kcs, q_str)]

    def softmax_update(s, m_old, l_old, start, masked):
        if masked:
            k_pos = {n: start + lax.broadcasted_iota(jnp.int32, (n, tw), 0) for n in set(x.shape[0] for x in s)}
            s = [jnp.where(k_pos[x.shape[0]] < lim, x, NEG_INF) for x, lim in zip(s, lim_str)]
        m_new = [jnp.maximum(m, jnp.max(x, axis=0, keepdims=True)) for m, x in zip(m_old, s)]
        corr = [jnp.exp2(m - mn) for m, mn in zip(m_old, m_new)]
        p = [jnp.exp2(x - mn) for x, mn in zip(s, m_new)]
        l_new = [l * c + jnp.sum(x, axis=0, keepdims=True) for l, c, x in zip(l_old, corr, p)]
        return m_new, corr, p, l_new

    def pv(vcts, p, acc_old, corr):
        return [a * c + jnp.dot(vct, x.astype(BF16), preferred_element_type=F32)
                for vct, a, c, x in zip(vcts, acc_old, corr, p)]

    def single(j, carry, masked):
        m_old, l_old, acc_old = carry
        start, kcs, vcts = tile(j, rows_diag if masked else rows_full)
        m_new, corr, p, l_new = softmax_update(qk(kcs), m_old, l_old, start, masked)
        return tuple(m_new), tuple(l_new), tuple(pv(vcts, p, acc_old, corr))

    def pair(j2, carry):
        m_old, l_old, acc_old = carry
        start_a, kc_a, vct_a = tile(2 * j2, rows_full)
        start_b, kc_b, vct_b = tile(2 * j2 + 1, rows_full)
        s_a = qk(kc_a)
        s_b = qk(kc_b)
        m_a, corr_a, p_a, l_a = softmax_update(s_a, m_old, l_old, start_a, False)
        acc_a = pv(vct_a, p_a, acc_old, corr_a)
        m_b, corr_b, p_b, l_b = softmax_update(s_b, m_a, l_a, start_b, False)
        return tuple(m_b), tuple(l_b), tuple(pv(vct_b, p_b, acc_a, corr_b))

    carry = (tuple(jnp.full((1, tw), NEG_INF, F32) for _ in range(n_str)),
             tuple(jnp.zeros((1, tw), F32) for _ in range(n_str)),
             tuple(jnp.zeros((LANES, tw), F32) for _ in range(n_str)))
    n_pair = n_full // 2
    carry = lax.fori_loop(0, n_pair, pair, carry)
    carry = lax.fori_loop(2 * n_pair, n_full, functools.partial(single, masked=False), carry)
    carry = lax.fori_loop(n_full, n_tot, functools.partial(single, masked=True), carry)
    _, l_fin, acc_fin = carry
    cat = lambda xs: xs[0] if len(xs) == 1 else jnp.concatenate(xs, axis=1)
    l1, l2 = cat(l_fin[:n_blk]), cat(l_fin[n_blk:])
    acc1, acc2 = cat(acc_fin[:n_blk]), cat(acc_fin[n_blk:])
    o_t = acc1 / l1 - lam * (acc2 / l2)
    ms = jnp.mean(o_t * o_t, axis=0, keepdims=True)
    o_t = o_t * lax.rsqrt(ms + EPS)
    o_ref[0] = (o_t.T * sub_ref[...] * (1.0 - lam_init)).astype(o_ref.dtype)


def diff_attn(q, k_all, v_all, past, t_valid, lq1, lk1, lq2, lk2, subln, lam_init, tq, tk):
    bsz, t_q, _ = q.shape
    t_kv = k_all.shape[1]
    assert t_q % tq == 0 and t_kv % tk == 0 and t_kv % LANES == 0 and tq % LANES == 0
    kern = functools.partial(_diff_attn_kernel, tq=tq, tk=tk, t_kv=t_kv, t_valid=t_valid,
                             past=past, lam_init=lam_init)
    vec = pl.BlockSpec((1, DF_HD), lambda b, h, i: (0, 0))
    kv = pl.BlockSpec((1, t_kv, LANES), lambda b, h, i: (b, 0, h))
    row = lambda a: a.reshape(1, -1)
    return pl.pallas_call(
        kern,
        grid=(bsz, DF_HEADS, t_q // tq),
        in_specs=[pl.BlockSpec((1, tq, LANES), lambda b, h, i: (b, i, h)), kv, kv,
                  vec, vec, vec, vec, pl.BlockSpec((1, DF_VD), lambda b, h, i: (0, 0))],
        out_specs=pl.BlockSpec((1, tq, LANES), lambda b, h, i: (b, i, h)),
        out_shape=jax.ShapeDtypeStruct((bsz, t_q, MIX_W), BF16),
        scratch_shapes=[pltpu.VMEM((t_kv, LANES), BF16), pltpu.VMEM((LANES, t_kv), BF16)],
        compiler_params=_params(("parallel", "parallel", "arbitrary")),
        name="diff_attn",
    )(q, k_all, v_all, row(lq1), row(lk1), row(lq2), row(lk2), row(subln))


def _gate_out_kernel(x_ref, ymix_ref, gq_ref, mk_ref, mv_ref, wout_ref, nw_ref, *out_refs):
    gq = gq_ref[0]
    gate = gq[:, :BRANCH].astype(F32)
    heads = []
    for h in range(MEM_HEADS):
        sl = slice(h * MEM_HD, (h + 1) * MEM_HD)
        qh = gq[:, BRANCH + h * MEM_HD:BRANCH + (h + 1) * MEM_HD]
        s = _mm_nt(qh, mk_ref[0][:, sl]) * (MEM_HD ** -0.5)
        e = jnp.exp(s - jnp.max(s, axis=-1, keepdims=True))
        prob = e / jnp.sum(e, axis=-1, keepdims=True)
        heads.append(_mm(prob, mv_ref[0][:, sl]))
    branch = jnp.concatenate([ymix_ref[0].astype(F32)] + heads, axis=-1)
    act = (branch * (gate * _sigmoid(gate))).astype(BF16)
    xn = x_ref[0] + jnp.dot(act, wout_ref[...], preferred_element_type=F32)
    normed_ref = out_refs[-1]
    normed_ref[0] = (_rms(xn) * nw_ref[...]).astype(normed_ref.dtype)
    if len(out_refs) == 2:
        out_refs[0][0] = xn


def gate_out(x, ymix, gq, mk, mv, w_out, next_norm_w, last):
    bsz, t, d = x.shape
    bm = min(GATE_ROWS, t)
    assert t % bm == 0
    tokb = lambda w: pl.BlockSpec((1, bm, w), lambda b, i: (b, i, 0))
    mem = pl.BlockSpec((1, N_MEM, MEM_W), lambda b, i: (b, 0, 0))
    if last:
        out_specs, out_shape = tokb(d), jax.ShapeDtypeStruct((bsz, t, d), F32)
    else:
        out_specs = [tokb(d), tokb(d)]
        out_shape = [jax.ShapeDtypeStruct((bsz, t, d), F32), jax.ShapeDtypeStruct((bsz, t, d), BF16)]
    return pl.pallas_call(
        _gate_out_kernel,
        grid=(bsz, t // bm),
        in_specs=[tokb(d), tokb(MIX_W), tokb(BRANCH + MEM_W), mem, mem,
                  pl.BlockSpec((BRANCH, d), lambda b, i: (0, 0)),
                  pl.BlockSpec((1, d), lambda b, i: (0, 0))],
        out_specs=out_specs,
        out_shape=out_shape,
        compiler_params=_params(("parallel", "parallel")),
        name="gate_out",
    )(x, ymix, gq, mk, mv, w_out, next_norm_w.reshape(1, d))


def _mem_kv(mem, norm_w, wk, wv):
    bsz = mem.shape[0]
    hm = rmsnorm_bf16(mem.reshape(bsz * N_MEM, D_MODEL), norm_w)
    mk = proj(hm, wk.astype(BF16), MEM_W, F32)
    mv = proj(hm, wv.astype(BF16), MEM_W, F32)
    return mk.reshape(bsz, N_MEM, MEM_W), mv.reshape(bsz, N_MEM, MEM_W)


def _rwkv_layer(x, h, mk, mv, prev_row, s0, w_out, next_norm_w, w_in, mu, w0, w_up, a0, a_up,
                k_k, k_a, r_k, ln_w, ln_b, ts, cl):
    bsz, t, d = x.shape
    w_p = jnp.pad(w_in[:, :RW_PROJ], ((0, 0), (0, RW_PROJ_PAD - RW_PROJ))).astype(BF16)
    w_gq = jnp.concatenate([w_in[:, RW_PROJ + MEM_W:], w_in[:, RW_PROJ:RW_PROJ + MEM_W]], axis=1).astype(BF16)
    p = proj(h, w_p, RW_PROJ_PAD // 2, BF16).reshape(bsz, t, RW_PROJ_PAD)
    gq = proj(h, w_gq, (BRANCH + MEM_W) // 2, BF16).reshape(bsz, t, BRANCH + MEM_W)
    p_last = proj(h.reshape(bsz, t, d)[:, t - 1], w_p, RW_PROJ_PAD // 2, F32)[:, :RW_PROJ]
    zero = jnp.zeros((LORA, MIX_W), F32)
    wup = jnp.concatenate([w_up, zero], axis=0).astype(BF16)
    aup = jnp.concatenate([zero, a_up], axis=0).astype(BF16)
    y_mix, z_t = rwkv_core(p, prev_row.reshape(bsz, 1, RW_PROJ), _pair_state_in(s0), mu, w0, a0, k_k, k_a,
                           r_k.reshape(-1), ln_w, ln_b, wup, aup, ts, cl)
    x_new, h_new = gate_out(x, y_mix, gq, mk, mv, w_out.astype(BF16), next_norm_w, last=False)
    return x_new, h_new.reshape(bsz * t, d), _pair_state_out(z_t), p_last


def _round_up(n, m):
    return (n + m - 1) // m * m


def _diff_layer(x, h, mk, mv, k_past, v_past, layer_idx, w_out, final_w, w_in, lq1, lk1, lq2, lk2,
                subln, tq, tk):
    bsz, t, d = x.shape
    cols = lambda lo, hi: w_in[:, lo:hi].astype(BF16)
    q = proj(h, cols(0, MIX_W), MIX_W, BF16).reshape(bsz, t, MIX_W)
    k = proj(h, cols(MIX_W, 2 * MIX_W), MIX_W, F32).reshape(bsz, t, MIX_W)
    v = proj(h, cols(2 * MIX_W, 3 * MIX_W), MIX_W, F32).reshape(bsz, t, MIX_W)
    w_gq = jnp.concatenate([w_in[:, 3 * MIX_W + MEM_W:], w_in[:, 3 * MIX_W:3 * MIX_W + MEM_W]], axis=1).astype(BF16)
    gq = proj(h, w_gq, (BRANCH + MEM_W) // 2, BF16).reshape(bsz, t, BRANCH + MEM_W)
    past = 0 if k_past is None else k_past.shape[1]
    t_valid = past + t
    t_q, t_kv = _round_up(t, tq), _round_up(t_valid, tk)
    pad_rows = lambda a, n: a if n == 0 else jnp.pad(a, ((0, 0), (0, n), (0, 0)))
    k_all, v_all = k, v
    if k_past is not None:
        k_all = jnp.concatenate([k_past.reshape(bsz, past, MIX_W), k], axis=1)
        v_all = jnp.concatenate([v_past.reshape(bsz, past, MIX_W), v], axis=1)
    lam_init = 0.8 - 0.6 * math.exp(-0.3 * layer_idx)
    o = diff_attn(pad_rows(q, t_q - t), pad_rows(k_all, t_kv - t_valid), pad_rows(v_all, t_kv - t_valid),
                  past, t_valid, lq1, lk1, lq2, lk2, subln, lam_init, tq, tk)[:, :t]
    y = gate_out(x, o, gq, mk, mv, w_out.astype(BF16), final_w, last=True)
    return y, k.reshape(bsz, t, DF_HEADS, 2 * DF_HD), v.reshape(bsz, t, DF_HEADS, DF_VD)


def kernel(x_prompt, mem_prompt, x_sample, state_rwkv, state_shift, cache_k, cache_v, cache_mem_k, cache_mem_v, norm_w, mem_norm_w, w_mem_k, w_mem_v, w_out, final_norm_w, rw_in, rw_mu, rw_w0, rw_w_up, rw_a0, rw_a_up, rw_k_k, rw_k_a, rw_r_k, rw_ln_w, rw_ln_b, df_in, df_lq1, df_lk1, df_lq2, df_lk2, df_subln):
    bp, tp, d = x_prompt.shape
    bs, tsm, _ = x_sample.shape
    mem_s = lambda m: m.reshape(bs, N_MEM, MEM_W)
    mk0, mv0 = _mem_kv(mem_prompt, mem_norm_w[0], w_mem_k[0], w_mem_v[0])
    mk1, mv1 = _mem_kv(mem_prompt, mem_norm_w[1], w_mem_k[1], w_mem_v[1])

    rw = (rw_in[0], rw_mu[0], rw_w0[0], rw_w_up[0], rw_a0[0], rw_a_up[0], rw_k_k[0], rw_k_a[0], rw_r_k[0],
          rw_ln_w[0], rw_ln_b[0])
    hp = rmsnorm_bf16(x_prompt.reshape(bp * tp, d), norm_w[0])
    xp, hp, p_s, p_shift = _rwkv_layer(
        x_prompt, hp, mk0, mv0, jnp.zeros((bp, RW_PROJ), F32), jnp.zeros((bp, RW_HEADS, RW_HD, RW_HD), F32),
        w_out[0], norm_w[1], *rw, ts=RW_CHUNK, cl=RW_CHUNK)
    hs = rmsnorm_bf16(x_sample.reshape(bs * tsm, d), norm_w[0])
    xs, hs, s_s, s_shift = _rwkv_layer(
        x_sample, hs, mem_s(cache_mem_k[0]), mem_s(cache_mem_v[0]), state_shift[0], state_rwkv[0],
        w_out[0], norm_w[1], *rw, ts=tsm, cl=tsm)

    df = (df_in[0], df_lq1[0], df_lk1[0], df_lq2[0], df_lk2[0], df_subln[0])
    y_prompt, p_k, p_v = _diff_layer(xp, hp, mk1, mv1, None, None, 1, w_out[1], final_norm_w, *df,
                                     tq=ATTN_TILE, tk=ATTN_TILE)
    t_kv_s = _round_up(cache_k.shape[2] + tsm, LANES)
    y_sample, s_k, s_v = _diff_layer(xs, hs, mem_s(cache_mem_k[1]), mem_s(cache_mem_v[1]), cache_k[0], cache_v[0], 1,
                                     w_out[1], final_norm_w, *df, tq=LANES, tk=t_kv_s)

    mem4 = lambda m: m.reshape(bp, N_MEM, MEM_HEADS, MEM_HD)
    return (y_prompt, y_sample, p_s[None], p_shift[None], p_k[None], p_v[None],
            jnp.stack([mem4(mk0), mem4(mk1)]), jnp.stack([mem4(mv0), mem4(mv1)]),
            s_s[None], s_shift[None], s_k[None], s_v[None])
```

```python
import functools
import math

import jax
import jax.numpy as jnp
from jax import lax
from jax.experimental import pallas as pl
from jax.experimental.pallas import tpu as pltpu

D_MODEL = 1024
CHUNK = 64
CHUNK_SHIFT = 6
N_MEM = 256
BRANCH = 2 * D_MODEL
MIX_W = 3 * D_MODEL // 2
MEM_W = D_MODEL // 2
MEM_HEADS = 4
MEM_HD = MEM_W // MEM_HEADS
RW_HD = 64
RW_HEADS = MIX_W // RW_HD
LORA = 64
RW_PROJ = 3 * MIX_W + 2 * LORA
DF_HD = 64
DF_HEADS = MIX_W // (2 * DF_HD)
DF_VD = 2 * DF_HD
EPS = 1e-6
GN_EPS = 64e-5
NEG_INF = -1e30
LOG2_E = 1.4426950408889634

LANES = 128
RW_PAIRS = RW_HEADS // 2
RW_PROJ_PAD = 4864
V7X_VMEM_LIMIT = 56 * 1024 * 1024

ROW_BLOCK = 1024
GATE_ROWS = 512
RW_CHUNK = 64
ATTN_Q_TILE = 1024
ATTN_K_TILE = 512

BF16 = jnp.bfloat16
F32 = jnp.float32


def _mm(a, b):
    return jnp.dot(a.astype(BF16), b.astype(BF16), preferred_element_type=F32)


def _mm_nt(a, b):
    return lax.dot_general(a.astype(BF16), b.astype(BF16), (((1,), (1,)), ((), ())),
                           preferred_element_type=F32)


def _mm_tn(a, b):
    return lax.dot_general(a.astype(BF16), b.astype(BF16), (((0,), (0,)), ((), ())),
                           preferred_element_type=F32)


def _sigmoid(x):
    return 0.5 * jnp.tanh(0.5 * x) + 0.5


def _softplus(x):
    return jnp.maximum(x, 0.0) + jnp.log(1.0 + jnp.exp(-jnp.abs(x)))


def _rms(x):
    return x * lax.rsqrt(jnp.mean(x * x, axis=-1, keepdims=True) + EPS)


def _params(sem):
    return pltpu.CompilerParams(dimension_semantics=sem, vmem_limit_bytes=V7X_VMEM_LIMIT)


def _rmsnorm_kernel(x_ref, nw_ref, o_ref):
    o_ref[...] = (_rms(x_ref[...]) * nw_ref[...]).astype(o_ref.dtype)


def rmsnorm_bf16(x, norm_w):
    n, d = x.shape
    bm = min(ROW_BLOCK, n)
    assert n % bm == 0
    return pl.pallas_call(
        _rmsnorm_kernel,
        grid=(n // bm,),
        in_specs=[pl.BlockSpec((bm, d), lambda i: (i, 0)), pl.BlockSpec((1, d), lambda i: (0, 0))],
        out_specs=pl.BlockSpec((bm, d), lambda i: (i, 0)),
        out_shape=jax.ShapeDtypeStruct((n, d), BF16),
        compiler_params=_params(("parallel",)),
        name="rmsnorm",
    )(x, norm_w.reshape(1, d))


def _proj_kernel(h_ref, w_ref, o_ref):
    o_ref[...] = jnp.dot(h_ref[...], w_ref[...], preferred_element_type=F32).astype(o_ref.dtype)


def proj(h, w, bn, out_dtype):
    n, d = h.shape
    nout = w.shape[1]
    bm = min(ROW_BLOCK, n)
    assert n % bm == 0 and nout % bn == 0
    return pl.pallas_call(
        _proj_kernel,
        grid=(nout // bn, n // bm),
        in_specs=[pl.BlockSpec((bm, d), lambda j, i: (i, 0)),
                  pl.BlockSpec((d, bn), lambda j, i: (0, j))],
        out_specs=pl.BlockSpec((bm, bn), lambda j, i: (i, j)),
        out_shape=jax.ShapeDtypeStruct((n, nout), out_dtype),
        compiler_params=_params(("parallel", "parallel")),
        name="proj",
    )(h, w)


def _proj_heads_kernel(h_ref, w_ref, o4_ref, ob_ref):
    res = jnp.dot(h_ref[...], w_ref[...], preferred_element_type=F32)
    ob_ref[...] = res.astype(BF16)
    for hh in range(DF_HEADS):
        o4_ref[:, hh, :] = res[:, hh * LANES:(hh + 1) * LANES]


def proj_heads(h, w):
    n, d = h.shape
    bm = min(ROW_BLOCK, n)
    assert n % bm == 0 and w.shape[1] == MIX_W
    return pl.pallas_call(
        _proj_heads_kernel,
        grid=(n // bm,),
        in_specs=[pl.BlockSpec((bm, d), lambda i: (i, 0)), pl.BlockSpec((d, MIX_W), lambda i: (0, 0))],
        out_specs=[pl.BlockSpec((bm, DF_HEADS, LANES), lambda i: (i, 0, 0)),
                   pl.BlockSpec((bm, MIX_W), lambda i: (i, 0))],
        out_shape=[jax.ShapeDtypeStruct((n, DF_HEADS, LANES), F32), jax.ShapeDtypeStruct((n, MIX_W), BF16)],
        compiler_params=_params(("parallel",)),
        name="proj_heads",
    )(h, w)


def _split3(x):
    hi = x.astype(BF16)
    r1 = x - hi.astype(F32)
    mid = r1.astype(BF16)
    lo = (r1 - mid.astype(F32)).astype(BF16)
    return hi, mid, lo


def _rwkv_kernel(r_ref, k_ref, v_ref, pp_ref, prev_ref, mu_ref, w0_ref, a0_ref, kk_ref, ka_ref, rk_ref,
                 lnw_ref, lnb_ref, wup_ref, aup_ref, z0_ref,
                 y_ref, zout_ref,
                 z_scr, carry, *, ts, cl):
    nc = ts // cl
    rb = 2 * cl
    c_idx = pl.program_id(1)
    k_off, v_off, pp_off = MIX_W, 2 * MIX_W, 3 * MIX_W

    @pl.when(c_idx == 0)
    def _init():
        z_scr[...] = z0_ref[0]
        carry[...] = prev_ref[0]

    def shifted(x_ref, lo):
        x = x_ref[0].astype(F32)
        hi = lo + x.shape[1]
        row = lax.broadcasted_iota(jnp.int32, x.shape, 0)
        prev = jnp.where(row == 0, carry[:, lo:hi], pltpu.roll(x, 1, 0))
        carry[:, lo:hi] = x[ts - 1:ts, :]
        return x + mu_ref[:, lo:hi] * (prev - x)

    r = shifted(r_ref, 0)
    k = shifted(k_ref, k_off)
    v = shifted(v_ref, v_off)
    pp = shifted(pp_ref, pp_off)

    lw = w0_ref[...] + _mm(jnp.tanh(pp), wup_ref[...])
    log_w = -_softplus(-lw) - 0.5
    d = -jnp.exp(log_w)
    a = _sigmoid(a0_ref[...] + _mm(pp, aup_ref[...]))

    li = lax.broadcasted_iota(jnp.int32, (LANES, LANES), 0)
    lj = lax.broadcasted_iota(jnp.int32, (LANES, LANES), 1)
    seg = jnp.where((li >> 6) == (lj >> 6), 1.0, 0.0).astype(BF16)
    pair_lanes = [slice(g * LANES, (g + 1) * LANES) for g in range(RW_PAIRS)]

    def head_sum(x):
        return jnp.concatenate([_mm(x[:, lg], seg) for lg in pair_lanes], axis=1)

    kk = k * kk_ref[...]
    kk = kk * lax.rsqrt(jnp.maximum(head_sum(kk * kk), 1e-24))
    k2 = k * (1.0 + (a - 1.0) * ka_ref[...])
    b = kk * a
    bonus = head_sum(r * k2 * rk_ref[...]) * v

    ci = lax.broadcasted_iota(jnp.int32, (cl, cl), 0)
    cj = lax.broadcasted_iota(jnp.int32, (cl, cl), 1)
    tril = jnp.where(ci >= cj, 1.0, 0.0).astype(BF16)
    rows = [slice(c0 * cl, (c0 + 1) * cl) for c0 in range(nc)]
    parts = _split3(d)
    cs_c = [sum(jnp.dot(tril, part[sl], preferred_element_type=F32) for part in parts) for sl in rows]
    pc_c = [jnp.exp(x[cl - 1:cl, :]) for x in cs_c]
    cat_rows = lambda xs: xs[0] if nc == 1 else jnp.concatenate(xs, axis=0)
    cs = cat_rows(cs_c)
    pc = cat_rows([jnp.broadcast_to(x, (cl, MIX_W)) for x in pc_c])
    e_nc = jnp.exp(-cs)
    at = -kk * jnp.exp(cs - d)
    rt = r * jnp.exp(cs)
    bt = b * e_nc
    kt = k2 * e_nc
    bh = bt * pc
    kh = kt * pc

    lane_c = lax.broadcasted_iota(jnp.int32, (cl, LANES), 1)
    first = lane_c < RW_HD
    si = lax.broadcasted_iota(jnp.int32, (rb, rb), 0)
    sj = lax.broadcasted_iota(jnp.int32, (rb, rb), 1)
    strict = si > sj
    incl = si >= sj
    eye = jnp.where(si == sj, 1.0, 0.0)

    def level_mask(sh):
        return (((si >> (sh + 1)) == (sj >> (sh + 1)))
                & (((si >> sh) & 1) == 1) & (((sj >> sh) & 1) == 0))

    n_levels = int(math.log2(cl))

    chains = [(sl, lg) for sl in rows for lg in pair_lanes]
    each = lambda fn, *lists: [fn(*xs) for xs in zip(*lists)]

    def stacked(x):
        def one(sl, lg):
            t = x[sl, lg]
            return jnp.concatenate([jnp.where(first, t, 0.0), jnp.where(first, 0.0, t)], axis=0)
        return [one(sl, lg) for sl, lg in chains]

    ats, rts, bts, kts, bhs, khs, vs = map(stacked, (at, rt, bt, kt, bh, kh, v))
    if rb == LANES:
        g = each(lambda a_, r_, b_, k_: _mm_nt(jnp.concatenate([a_, r_], axis=0),
                                               jnp.concatenate([b_, k_], axis=0)), ats, rts, bts, kts)
        g_ab, g_ak = [x[:rb, :rb] for x in g], [x[:rb, rb:] for x in g]
        g_rb, g_rk = [x[rb:, :rb] for x in g], [x[rb:, rb:] for x in g]
    else:
        g_ab, g_ak = each(_mm_nt, ats, bts), each(_mm_nt, ats, kts)
        g_rb, g_rk = each(_mm_nt, rts, bts), each(_mm_nt, rts, kts)
    a_ab = each(lambda x: jnp.where(strict, x, 0.0), g_ab)
    a_ak = each(lambda x: jnp.where(strict, x, 0.0), g_ak)
    a_rb = each(lambda x: jnp.where(incl, x, 0.0), g_rb)
    a_rk = each(lambda x: jnp.where(incl, x, 0.0), g_rk)
    t = each(lambda x: eye + jnp.where(level_mask(0), x, 0.0), a_ab)
    for sh in range(1, n_levels):
        lm = level_mask(sh)
        ta = each(lambda t_, x: _mm(t_, jnp.where(lm, x, 0.0)), t, a_ab)
        t = each(lambda t_, x: t_ + _mm(x, t_), t, ta)
    w = each(_mm, a_ak, vs)
    tw = each(lambda t_, a_, w_: _mm(t_, jnp.concatenate([a_, w_], axis=1)), t, ats, w)
    ap, vp = [x[:, :LANES] for x in tw], [x[:, LANES:] for x in tw]
    rp = each(lambda r_, a_, x: r_ + _mm(a_, x), rts, a_rb, ap)
    if rb == LANES:
        yp = each(lambda a_, c_, x, y_: _mm(jnp.concatenate([a_, c_], axis=1),
                                            jnp.concatenate([x, y_], axis=0)), a_rb, a_rk, vp, vs)
    else:
        yp = each(lambda a_, c_, x, y_: _mm(a_, x) + _mm(c_, y_), a_rb, a_rk, vp, vs)
    mn = each(_mm_tn, bhs, tw)
    diag = li == lj
    pcs = [x[:, lg] for x in pc_c for lg in pair_lanes]
    mm = each(lambda x, p: jnp.where(diag, jnp.broadcast_to(p, (LANES, LANES)), 0.0) + x[:, :LANES], mn, pcs)
    nn = each(lambda x, k_, y_: x[:, LANES:] + _mm_tn(k_, y_), mn, khs, vs)

    z = [z_scr[gi] for gi in range(RW_PAIRS)]
    y_rows = []
    for c0 in range(nc):
        base = c0 * RW_PAIRS
        y_st = [_mm(rp[base + gi], z[gi]) + yp[base + gi] for gi in range(RW_PAIRS)]
        z = [_mm(mm[base + gi], z[gi]) + nn[base + gi] for gi in range(RW_PAIRS)]
        y_rows.append(jnp.concatenate([x[:cl] + x[cl:] for x in y_st], axis=1))
    for gi in range(RW_PAIRS):
        z_scr[gi] = z[gi]
    y = cat_rows(y_rows)

    inv_hd = 1.0 / RW_HD
    yc = y - head_sum(y) * inv_hd
    var = head_sum(yc * yc) * inv_hd
    y_ref[0] = (yc * lax.rsqrt(var + GN_EPS) * lnw_ref[...] + lnb_ref[...] + bonus).astype(y_ref.dtype)

    @pl.when(c_idx == pl.num_programs(1) - 1)
    def _fin():
        for gi in range(RW_PAIRS):
            zout_ref[0, gi] = z[gi]


def rwkv_core(p, prev_row, z0, mu, w0, a0, k_k, k_a, r_k, ln_w, ln_b, wup, aup, ts, cl):
    bsz, t, _ = p.shape
    assert t % ts == 0 and ts % cl == 0
    tok = lambda off: pl.BlockSpec((1, ts, MIX_W), lambda b, c, off=off: (b, c, off))
    chan = pl.BlockSpec((1, MIX_W), lambda b, c: (0, 0))
    lora = pl.BlockSpec((LANES, MIX_W), lambda b, c: (0, 0))
    state = pl.BlockSpec((1, RW_PAIRS, LANES, LANES), lambda b, c: (b, 0, 0, 0))
    in_specs = [tok(0), tok(1), tok(2),
                pl.BlockSpec((1, ts, LANES), lambda b, c: (b, c, 3 * MIX_W // LANES)),
                pl.BlockSpec((1, 1, RW_PROJ), lambda b, c: (b, 0, 0)),
                pl.BlockSpec((1, RW_PROJ), lambda b, c: (0, 0)),
                chan, chan, chan, chan, chan, chan, chan, lora, lora, state]
    row = lambda a: a.reshape(1, -1)
    return pl.pallas_call(
        functools.partial(_rwkv_kernel, ts=ts, cl=cl),
        grid=(bsz, t // ts),
        in_specs=in_specs,
        out_specs=[pl.BlockSpec((1, ts, MIX_W), lambda b, c: (b, c, 0)), state],
        out_shape=[jax.ShapeDtypeStruct((bsz, t, MIX_W), BF16),
                   jax.ShapeDtypeStruct((bsz, RW_PAIRS, LANES, LANES), F32)],
        scratch_shapes=[pltpu.VMEM((RW_PAIRS, LANES, LANES), F32), pltpu.VMEM((1, RW_PROJ), F32)],
        compiler_params=_params(("parallel", "arbitrary")),
        name="rwkv_core",
    )(p, p, p, p, prev_row, row(mu), row(w0), row(a0), row(k_k), row(k_a), row(r_k), row(ln_w), row(ln_b),
      wup, aup, z0)


def _pair_state_in(s):
    bsz = s.shape[0]
    st = jnp.swapaxes(s.astype(F32), -1, -2).reshape(bsz, RW_PAIRS, 2, RW_HD, RW_HD)
    z = st[:, :, :, :, None, :] * jnp.eye(2, dtype=F32)[None, None, :, None, :, None]
    return z.reshape(bsz, RW_PAIRS, LANES, LANES)


def _pair_state_out(z):
    bsz = z.shape[0]
    z6 = z.reshape(bsz, RW_PAIRS, 2, RW_HD, 2, RW_HD)
    st = jnp.stack([z6[:, :, 0, :, 0, :], z6[:, :, 1, :, 1, :]], axis=2)
    return jnp.swapaxes(st.reshape(bsz, RW_HEADS, RW_HD, RW_HD), -1, -2)


def _attn_stream_width(tq):
    return min(tq, 2 * LANES)


def _diff_attn_kernel(q_ref, k_ref, v_ref, lq1, lk1, lq2, lk2, sub_ref, o_ref, vbt,
                      *, tq, tk, t_kv, t_valid, past, lam_init):
    qi = pl.program_id(2)
    kb = k_ref.at[0]

    @pl.when(qi == 0)
    def _stage():
        ei = lax.broadcasted_iota(jnp.int32, (LANES, LANES), 0)
        ej = lax.broadcasted_iota(jnp.int32, (LANES, LANES), 1)
        eye = jnp.where(ei == ej, 1.0, 0.0).astype(BF16)

        def body(c, _):
            start = pl.multiple_of(c * tk, tk)
            vt = lax.dot_general(eye, v_ref[0, pl.ds(start, tk), :], (((1,), (1,)), ((), ())),
                                 preferred_element_type=F32)
            vbt[:, pl.ds(start, tk)] = vt.astype(BF16)
            return 0

        lax.fori_loop(0, t_kv // tk, body, 0)

    lam = (jnp.exp(jnp.sum(lq1[...] * lk1[...], axis=-1, keepdims=True))
           - jnp.exp(jnp.sum(lq2[...] * lk2[...], axis=-1, keepdims=True)) + lam_init)

    lane = lax.broadcasted_iota(jnp.int32, (tq, LANES), 1)
    q = q_ref[0].astype(F32) * (DF_HD ** -0.5 * LOG2_E)
    q1 = jnp.where(lane < DF_HD, q, 0.0).astype(BF16)
    q2 = jnp.where(lane >= DF_HD, q, 0.0).astype(BF16)

    q0 = past + qi * tq
    lim_first = jnp.minimum(((q0 >> CHUNK_SHIFT) + 1) << CHUNK_SHIFT, t_valid)
    lim_last = jnp.minimum((((q0 + tq - 1) >> CHUNK_SHIFT) + 1) << CHUNK_SHIFT, t_valid)
    n_full = lim_first // tk
    n_tot = (lim_last + tk - 1) // tk

    tw = _attn_stream_width(tq)
    n_blk = tq // tw
    n_str = 2 * n_blk
    q_str = [qh[i * tw:(i + 1) * tw] for qh in (q1, q2) for i in range(n_blk)]
    q_pos = q0 + lax.broadcasted_iota(jnp.int32, (1, tq), 1)
    q_lim = jnp.minimum(((q_pos >> CHUNK_SHIFT) + 1) << CHUNK_SHIFT, t_valid)
    lim_str = [q_lim[:, i * tw:(i + 1) * tw] for _ in range(2) for i in range(n_blk)]
    rows_full = [tk] * n_str

    def tile(j, rows):
        start = pl.multiple_of(j * tk, tk)
        uniq = {n: (kb[pl.ds(start, n), :], vbt[:, pl.ds(start, n)]) for n in sorted(set(rows)) if n}
        uniq[0] = (None, None)
        return start, [uniq[n][0] for n in rows], [uniq[n][1] for n in rows]

    def qk(kcs):
        return [None if kc is None else
                lax.dot_general(kc, qs, (((1,), (1,)), ((), ())), preferred_element_type=F32)
                for kc, qs in zip(kcs, q_str)]

    def softmax_update(s, m_old, l_old, start, masked):
        live = [x is not None for x in s]
        if masked:
            k_pos = {n: start + lax.broadcasted_iota(jnp.int32, (n, tw), 0)
                     for n in set(x.shape[0] for x in s if x is not None)}
            s = [jnp.where(k_pos[x.shape[0]] < lim, x, NEG_INF) if ok else None
                 for ok, x, lim in zip(live, s, lim_str)]
        m_new = [jnp.maximum(m, jnp.max(x, axis=0, keepdims=True)) if ok else m
                 for ok, m, x in zip(live, m_old, s)]
        corr = [jnp.exp2(m - mn) if ok else None for ok, m, mn in zip(live, m_old, m_new)]
        p = [jnp.exp2(x - mn) if ok else None for ok, x, mn in zip(live, s, m_new)]
        l_new = [l * c + jnp.sum(x, axis=0, keepdims=True) if ok else l
                 for ok, l, c, x in zip(live, l_old, corr, p)]
        return m_new, corr, p, l_new

    def pv(vcts, p, acc_old, corr):
        return [a if x is None else a * c + jnp.dot(vct, x.astype(BF16), preferred_element_type=F32)
                for vct, a, c, x in zip(vcts, acc_old, corr, p)]

    def single(j, carry, masked, rows=rows_full):
        m_old, l_old, acc_old = carry
        start, kcs, vcts = tile(j, rows)
        m_new, corr, p, l_new = softmax_update(qk(kcs), m_old, l_old, start, masked)
        return tuple(m_new), tuple(l_new), tuple(pv(vcts, p, acc_old, corr))

    def pair(j2, carry):
        m_old, l_old, acc_old = carry
        start_a, kc_a, vct_a = tile(2 * j2, rows_full)
        start_b, kc_b, vct_b = tile(2 * j2 + 1, rows_full)
        s_a = qk(kc_a)
        s_b = qk(kc_b)
        m_a, corr_a, p_a, l_a = softmax_update(s_a, m_old, l_old, start_a, False)
        acc_a = pv(vct_a, p_a, acc_old, corr_a)
        m_b, corr_b, p_b, l_b = softmax_update(s_b, m_a, l_a, start_b, False)
        return tuple(m_b), tuple(l_b), tuple(pv(vct_b, p_b, acc_a, corr_b))

    carry = (tuple(jnp.full((1, tw), NEG_INF, F32) for _ in range(n_str)),
             tuple(jnp.zeros((1, tw), F32) for _ in range(n_str)),
             tuple(jnp.zeros((LANES, tw), F32) for _ in range(n_str)))
    n_pair = n_full // 2
    carry = lax.fori_loop(0, n_pair, pair, carry)
    carry = lax.fori_loop(2 * n_pair, n_full, functools.partial(single, masked=False), carry)
    if past == 0 and tq % tk == 0 and t_valid == t_kv:
        for dt in range(tq // tk):
            rows = [min(max((i + 1) * tw - dt * tk, 0), tk) for _ in range(2) for i in range(n_blk)]
            carry = single(n_full + dt, carry, True, rows)
    else:
        carry = lax.fori_loop(n_full, n_tot, functools.partial(single, masked=True), carry)
    _, l_fin, acc_fin = carry
    cat = lambda xs: xs[0] if len(xs) == 1 else jnp.concatenate(xs, axis=1)
    l1, l2 = cat(l_fin[:n_blk]), cat(l_fin[n_blk:])
    acc1, acc2 = cat(acc_fin[:n_blk]), cat(acc_fin[n_blk:])
    o_t = acc1 * (1.0 / l1) - acc2 * (lam / l2)
    ms = jnp.mean(o_t * o_t, axis=0, keepdims=True)
    o_t = o_t * lax.rsqrt(ms + EPS)
    o_ref[0] = (o_t.T * sub_ref[...] * (1.0 - lam_init)).astype(o_ref.dtype)


def diff_attn(q, k_all, v_all, past, t_valid, lq1, lk1, lq2, lk2, subln, lam_init, tq, tk):
    bsz, t_q, _ = q.shape
    t_kv = k_all.shape[1]
    assert t_q % tq == 0 and t_kv % tk == 0 and t_kv % LANES == 0 and tq % LANES == 0
    kern = functools.partial(_diff_attn_kernel, tq=tq, tk=tk, t_kv=t_kv, t_valid=t_valid,
                             past=past, lam_init=lam_init)
    vec = pl.BlockSpec((1, DF_HD), lambda b, h, i: (0, 0))
    kv = pl.BlockSpec((1, t_kv, LANES), lambda b, h, i: (b, 0, h))
    row = lambda a: a.reshape(1, -1)
    return pl.pallas_call(
        kern,
        grid=(bsz, DF_HEADS, t_q // tq),
        in_specs=[pl.BlockSpec((1, tq, LANES), lambda b, h, i: (b, i, h)), kv, kv,
                  vec, vec, vec, vec, pl.BlockSpec((1, DF_VD), lambda b, h, i: (0, 0))],
        out_specs=pl.BlockSpec((1, tq, LANES), lambda b, h, i: (b, i, h)),
        out_shape=jax.ShapeDtypeStruct((bsz, t_q, MIX_W), BF16),
        scratch_shapes=[pltpu.VMEM((LANES, t_kv), BF16)],
        compiler_params=_params(("parallel", "parallel", "arbitrary")),
        name="diff_attn",
    )(q, k_all, v_all, row(lq1), row(lk1), row(lq2), row(lk2), row(subln))


def _gate_out_kernel(x_ref, ymix_ref, gq_ref, mk_ref, mv_ref, wout_ref, nw_ref, *out_refs):
    gq = gq_ref[0]
    gate = gq[:, :BRANCH].astype(F32)
    heads = []
    for h in range(MEM_HEADS):
        sl = slice(h * MEM_HD, (h + 1) * MEM_HD)
        qh = gq[:, BRANCH + h * MEM_HD:BRANCH + (h + 1) * MEM_HD]
        s = _mm_nt(qh, mk_ref[0][:, sl]) * (MEM_HD ** -0.5)
        e = jnp.exp(s - jnp.max(s, axis=-1, keepdims=True))
        prob = e * (1.0 / jnp.sum(e, axis=-1, keepdims=True))
        heads.append(_mm(prob, mv_ref[0][:, sl]))
    branch = jnp.concatenate([ymix_ref[0].astype(F32)] + heads, axis=-1)
    act = (branch * (gate * _sigmoid(gate))).astype(BF16)
    xn = x_ref[0] + jnp.dot(act, wout_ref[...], preferred_element_type=F32)
    normed_ref = out_refs[-1]
    normed_ref[0] = (_rms(xn) * nw_ref[...]).astype(normed_ref.dtype)
    if len(out_refs) == 2:
        out_refs[0][0] = xn


def gate_out(x, ymix, gq, mk, mv, w_out, next_norm_w, last):
    bsz, t, d = x.shape
    bm = min(GATE_ROWS, t)
    assert t % bm == 0
    tokb = lambda w: pl.BlockSpec((1, bm, w), lambda b, i: (b, i, 0))
    mem = pl.BlockSpec((1, N_MEM, MEM_W), lambda b, i: (b, 0, 0))
    if last:
        out_specs, out_shape = tokb(d), jax.ShapeDtypeStruct((bsz, t, d), F32)
    else:
        out_specs = [tokb(d), tokb(d)]
        out_shape = [jax.ShapeDtypeStruct((bsz, t, d), F32), jax.ShapeDtypeStruct((bsz, t, d), BF16)]
    return pl.pallas_call(
        _gate_out_kernel,
        grid=(bsz, t // bm),
        in_specs=[tokb(d), tokb(MIX_W), tokb(BRANCH + MEM_W), mem, mem,
                  pl.BlockSpec((BRANCH, d), lambda b, i: (0, 0)),
                  pl.BlockSpec((1, d), lambda b, i: (0, 0))],
        out_specs=out_specs,
        out_shape=out_shape,
        compiler_params=_params(("parallel", "parallel")),
        name="gate_out",
    )(x, ymix, gq, mk, mv, w_out, next_norm_w.reshape(1, d))


def _mem_kv(mem, norm_w, wk, wv):
    bsz = mem.shape[0]
    hm = rmsnorm_bf16(mem.reshape(bsz * N_MEM, D_MODEL), norm_w)
    mk = proj(hm, wk.astype(BF16), MEM_W, F32)
    mv = proj(hm, wv.astype(BF16), MEM_W, F32)
    return mk.reshape(bsz, N_MEM, MEM_W), mv.reshape(bsz, N_MEM, MEM_W)


def _rwkv_layer(x, h, mk, mv, prev_row, s0, w_out, next_norm_w, w_in, mu, w0, w_up, a0, a_up,
                k_k, k_a, r_k, ln_w, ln_b, ts, cl):
    bsz, t, d = x.shape
    w_p = jnp.pad(w_in[:, :RW_PROJ], ((0, 0), (0, RW_PROJ_PAD - RW_PROJ))).astype(BF16)
    w_gq = jnp.concatenate([w_in[:, RW_PROJ + MEM_W:], w_in[:, RW_PROJ:RW_PROJ + MEM_W]], axis=1).astype(BF16)
    p = proj(h, w_p, RW_PROJ_PAD // 2, BF16).reshape(bsz, t, RW_PROJ_PAD)
    gq = proj(h, w_gq, (BRANCH + MEM_W) // 2, BF16).reshape(bsz, t, BRANCH + MEM_W)
    p_last = proj(h.reshape(bsz, t, d)[:, t - 1], w_p, RW_PROJ_PAD // 2, F32)[:, :RW_PROJ]
    zero = jnp.zeros((LORA, MIX_W), F32)
    wup = jnp.concatenate([w_up, zero], axis=0).astype(BF16)
    aup = jnp.concatenate([zero, a_up], axis=0).astype(BF16)
    y_mix, z_t = rwkv_core(p, prev_row.reshape(bsz, 1, RW_PROJ), _pair_state_in(s0), mu, w0, a0, k_k, k_a,
                           r_k.reshape(-1), ln_w, ln_b, wup, aup, ts, cl)
    x_new, h_new = gate_out(x, y_mix, gq, mk, mv, w_out.astype(BF16), next_norm_w, last=False)
    return x_new, h_new.reshape(bsz * t, d), _pair_state_out(z_t), p_last


def _round_up(n, m):
    return (n + m - 1) // m * m


def _diff_layer(x, h, mk, mv, k_past, v_past, layer_idx, w_out, final_w, w_in, lq1, lk1, lq2, lk2,
                subln, tq, tk):
    bsz, t, d = x.shape
    cols = lambda lo, hi: w_in[:, lo:hi].astype(BF16)
    q = proj(h, cols(0, MIX_W), MIX_W, BF16).reshape(bsz, t, MIX_W)
    k4, k = proj_heads(h, cols(MIX_W, 2 * MIX_W))
    v4, v = proj_heads(h, cols(2 * MIX_W, 3 * MIX_W))
    k, v = k.reshape(bsz, t, MIX_W), v.reshape(bsz, t, MIX_W)
    w_gq = jnp.concatenate([w_in[:, 3 * MIX_W + MEM_W:], w_in[:, 3 * MIX_W:3 * MIX_W + MEM_W]], axis=1).astype(BF16)
    gq = proj(h, w_gq, (BRANCH + MEM_W) // 2, BF16).reshape(bsz, t, BRANCH + MEM_W)
    past = 0 if k_past is None else k_past.shape[1]
    t_valid = past + t
    t_q, t_kv = _round_up(t, tq), _round_up(t_valid, tk)
    pad_rows = lambda a, n: a if n == 0 else jnp.pad(a, ((0, 0), (0, n), (0, 0)))
    k_all, v_all = k, v
    if k_past is not None:
        k_all = jnp.concatenate([k_past.reshape(bsz, past, MIX_W).astype(BF16), k], axis=1)
        v_all = jnp.concatenate([v_past.reshape(bsz, past, MIX_W).astype(BF16), v], axis=1)
    lam_init = 0.8 - 0.6 * math.exp(-0.3 * layer_idx)
    o = diff_attn(pad_rows(q, t_q - t), pad_rows(k_all, t_kv - t_valid), pad_rows(v_all, t_kv - t_valid),
                  past, t_valid, lq1, lk1, lq2, lk2, subln, lam_init, tq, tk)[:, :t]
    y = gate_out(x, o, gq, mk, mv, w_out.astype(BF16), final_w, last=True)
    return y, k4.reshape(bsz, t, DF_HEADS, 2 * DF_HD), v4.reshape(bsz, t, DF_HEADS, DF_VD)


def kernel(x_prompt, mem_prompt, x_sample, state_rwkv, state_shift, cache_k, cache_v, cache_mem_k, cache_mem_v, norm_w, mem_norm_w, w_mem_k, w_mem_v, w_out, final_norm_w, rw_in, rw_mu, rw_w0, rw_w_up, rw_a0, rw_a_up, rw_k_k, rw_k_a, rw_r_k, rw_ln_w, rw_ln_b, df_in, df_lq1, df_lk1, df_lq2, df_lk2, df_subln):
    bp, tp, d = x_prompt.shape
    bs, tsm, _ = x_sample.shape
    mem_s = lambda m: m.reshape(bs, N_MEM, MEM_W)
    mk0, mv0 = _mem_kv(mem_prompt, mem_norm_w[0], w_mem_k[0], w_mem_v[0])
    mk1, mv1 = _mem_kv(mem_prompt, mem_norm_w[1], w_mem_k[1], w_mem_v[1])

    rw = (rw_in[0], rw_mu[0], rw_w0[0], rw_w_up[0], rw_a0[0], rw_a_up[0], rw_k_k[0], rw_k_a[0], rw_r_k[0],
          rw_ln_w[0], rw_ln_b[0])
    hp = rmsnorm_bf16(x_prompt.reshape(bp * tp, d), norm_w[0])
    xp, hp, p_s, p_shift = _rwkv_layer(
        x_prompt, hp, mk0, mv0, jnp.zeros((bp, RW_PROJ), F32), jnp.zeros((bp, RW_HEADS, RW_HD, RW_HD), F32),
        w_out[0], norm_w[1], *rw, ts=RW_CHUNK, cl=RW_CHUNK)
    hs = rmsnorm_bf16(x_sample.reshape(bs * tsm, d), norm_w[0])
    xs, hs, s_s, s_shift = _rwkv_layer(
        x_sample, hs, mem_s(cache_mem_k[0]), mem_s(cache_mem_v[0]), state_shift[0], state_rwkv[0],
        w_out[0], norm_w[1], *rw, ts=tsm, cl=tsm)

    df = (df_in[0], df_lq1[0], df_lk1[0], df_lq2[0], df_lk2[0], df_subln[0])
    y_prompt, p_k, p_v = _diff_layer(xp, hp, mk1, mv1, None, None, 1, w_out[1], final_norm_w, *df,
                                     tq=ATTN_Q_TILE, tk=ATTN_K_TILE)
    t_kv_s = _round_up(cache_k.shape[2] + tsm, LANES)
    y_sample, s_k, s_v = _diff_layer(xs, hs, mem_s(cache_mem_k[1]), mem_s(cache_mem_v[1]), cache_k[0], cache_v[0], 1,
                                     w_out[1], final_norm_w, *df, tq=LANES, tk=t_kv_s)

    mem4 = lambda m: m.reshape(bp, N_MEM, MEM_HEADS, MEM_HD)
    return (y_prompt, y_sample, p_s[None], p_shift[None], p_k[None], p_v[None],
            jnp.stack([mem4(mk0), mem4(mk1)]), jnp.stack([mem4(mv0), mem4(mv1)]),
            s_s[None], s_shift[None], s_k[None], s_v[None])
```

```python
import functools
import math

import jax
import jax.numpy as jnp
from jax import lax
from jax.experimental import pallas as pl
from jax.experimental.pallas import tpu as pltpu

D_MODEL = 1024
CHUNK = 64
CHUNK_SHIFT = 6
N_MEM = 256
BRANCH = 2 * D_MODEL
MIX_W = 3 * D_MODEL // 2
MEM_W = D_MODEL // 2
MEM_HEADS = 4
MEM_HD = MEM_W // MEM_HEADS
RW_HD = 64
RW_HEADS = MIX_W // RW_HD
LORA = 64
RW_PROJ = 3 * MIX_W + 2 * LORA
DF_HD = 64
DF_HEADS = MIX_W // (2 * DF_HD)
DF_VD = 2 * DF_HD
EPS = 1e-6
GN_EPS = 64e-5
NEG_INF = -1e30
LOG2_E = 1.4426950408889634

LANES = 128
RW_PAIRS = RW_HEADS // 2
RW_PROJ_PAD = 4864
V7X_VMEM_LIMIT = 56 * 1024 * 1024

ROW_BLOCK = 1024
GATE_ROWS = 512
RW_CHUNK = 64
RW_STEP = 2 * RW_CHUNK
ATTN_Q_TILE = 1024
ATTN_K_TILE = 512

BF16 = jnp.bfloat16
F32 = jnp.float32


def _mm(a, b):
    return jnp.dot(a.astype(BF16), b.astype(BF16), preferred_element_type=F32)


def _mm_nt(a, b):
    return lax.dot_general(a.astype(BF16), b.astype(BF16), (((1,), (1,)), ((), ())),
                           preferred_element_type=F32)


def _mm_tn(a, b):
    return lax.dot_general(a.astype(BF16), b.astype(BF16), (((0,), (0,)), ((), ())),
                           preferred_element_type=F32)


def _sigmoid(x):
    return 0.5 * jnp.tanh(0.5 * x) + 0.5


def _softplus(x):
    return jnp.maximum(x, 0.0) + jnp.log(1.0 + jnp.exp(-jnp.abs(x)))


def _rms(x):
    return x * lax.rsqrt(jnp.mean(x * x, axis=-1, keepdims=True) + EPS)


def _params(sem):
    return pltpu.CompilerParams(dimension_semantics=sem, vmem_limit_bytes=V7X_VMEM_LIMIT)


def _rmsnorm_kernel(x_ref, nw_ref, o_ref):
    o_ref[...] = (_rms(x_ref[...]) * nw_ref[...]).astype(o_ref.dtype)


def rmsnorm_bf16(x, norm_w):
    n, d = x.shape
    bm = min(ROW_BLOCK, n)
    assert n % bm == 0
    return pl.pallas_call(
        _rmsnorm_kernel,
        grid=(n // bm,),
        in_specs=[pl.BlockSpec((bm, d), lambda i: (i, 0)), pl.BlockSpec((1, d), lambda i: (0, 0))],
        out_specs=pl.BlockSpec((bm, d), lambda i: (i, 0)),
        out_shape=jax.ShapeDtypeStruct((n, d), BF16),
        compiler_params=_params(("parallel",)),
        name="rmsnorm",
    )(x, norm_w.reshape(1, d))


def _proj_kernel(h_ref, w_ref, o_ref):
    o_ref[...] = jnp.dot(h_ref[...], w_ref[...], preferred_element_type=F32).astype(o_ref.dtype)


def proj(h, w, bn, out_dtype):
    n, d = h.shape
    nout = w.shape[1]
    bm = min(ROW_BLOCK, n)
    assert n % bm == 0 and nout % bn == 0
    return pl.pallas_call(
        _proj_kernel,
        grid=(nout // bn, n // bm),
        in_specs=[pl.BlockSpec((bm, d), lambda j, i: (i, 0)),
                  pl.BlockSpec((d, bn), lambda j, i: (0, j))],
        out_specs=pl.BlockSpec((bm, bn), lambda j, i: (i, j)),
        out_shape=jax.ShapeDtypeStruct((n, nout), out_dtype),
        compiler_params=_params(("parallel", "parallel")),
        name="proj",
    )(h, w)


def _proj_heads_kernel(h_ref, w_ref, o4_ref, ob_ref):
    res = jnp.dot(h_ref[...], w_ref[...], preferred_element_type=F32)
    ob_ref[...] = res.astype(BF16)
    for hh in range(DF_HEADS):
        o4_ref[:, hh, :] = res[:, hh * LANES:(hh + 1) * LANES]


def proj_heads(h, w):
    n, d = h.shape
    bm = min(ROW_BLOCK, n)
    assert n % bm == 0 and w.shape[1] == MIX_W
    return pl.pallas_call(
        _proj_heads_kernel,
        grid=(n // bm,),
        in_specs=[pl.BlockSpec((bm, d), lambda i: (i, 0)), pl.BlockSpec((d, MIX_W), lambda i: (0, 0))],
        out_specs=[pl.BlockSpec((bm, DF_HEADS, LANES), lambda i: (i, 0, 0)),
                   pl.BlockSpec((bm, MIX_W), lambda i: (i, 0))],
        out_shape=[jax.ShapeDtypeStruct((n, DF_HEADS, LANES), F32), jax.ShapeDtypeStruct((n, MIX_W), BF16)],
        compiler_params=_params(("parallel",)),
        name="proj_heads",
    )(h, w)


def _split3(x):
    hi = x.astype(BF16)
    r1 = x - hi.astype(F32)
    mid = r1.astype(BF16)
    lo = (r1 - mid.astype(F32)).astype(BF16)
    return hi, mid, lo


def _rwkv_kernel(r_ref, k_ref, v_ref, pp_ref, prev_ref, mu_ref, w0_ref, a0_ref, kk_ref, ka_ref, rk_ref,
                 lnw_ref, lnb_ref, wup_ref, aup_ref, z0_ref,
                 y_ref, zout_ref,
                 z_scr, carry, *, ts, cl):
    nc = ts // cl
    rb = 2 * cl
    c_idx = pl.program_id(1)
    k_off, v_off, pp_off = MIX_W, 2 * MIX_W, 3 * MIX_W

    @pl.when(c_idx == 0)
    def _init():
        z_scr[...] = z0_ref[0]
        carry[...] = prev_ref[0]

    def shifted(x_ref, lo):
        x = x_ref[0].astype(F32)
        hi = lo + x.shape[1]
        row = lax.broadcasted_iota(jnp.int32, x.shape, 0)
        prev = jnp.where(row == 0, carry[:, lo:hi], pltpu.roll(x, 1, 0))
        carry[:, lo:hi] = x[ts - 1:ts, :]
        return x + mu_ref[:, lo:hi] * (prev - x)

    r = shifted(r_ref, 0)
    k = shifted(k_ref, k_off)
    v = shifted(v_ref, v_off)
    pp = shifted(pp_ref, pp_off)

    lw = w0_ref[...] + _mm(jnp.tanh(pp), wup_ref[...])
    log_w = -_softplus(-lw) - 0.5
    d = -jnp.exp(log_w)
    a = _sigmoid(a0_ref[...] + _mm(pp, aup_ref[...]))

    li = lax.broadcasted_iota(jnp.int32, (LANES, LANES), 0)
    lj = lax.broadcasted_iota(jnp.int32, (LANES, LANES), 1)
    seg = jnp.where((li >> 6) == (lj >> 6), 1.0, 0.0).astype(BF16)
    pair_lanes = [slice(g * LANES, (g + 1) * LANES) for g in range(RW_PAIRS)]

    def head_sum(x):
        return jnp.concatenate([_mm(x[:, lg], seg) for lg in pair_lanes], axis=1)

    kk = k * kk_ref[...]
    kk = kk * lax.rsqrt(jnp.maximum(head_sum(kk * kk), 1e-24))
    k2 = k * (1.0 + (a - 1.0) * ka_ref[...])
    b = kk * a
    bonus = head_sum(r * k2 * rk_ref[...]) * v

    ci = lax.broadcasted_iota(jnp.int32, (cl, cl), 0)
    cj = lax.broadcasted_iota(jnp.int32, (cl, cl), 1)
    tril = jnp.where(ci >= cj, 1.0, 0.0).astype(BF16)
    rows = [slice(c0 * cl, (c0 + 1) * cl) for c0 in range(nc)]
    parts = _split3(d)
    cs_c = [sum(jnp.dot(tril, part[sl], preferred_element_type=F32) for part in parts) for sl in rows]
    pc_c = [jnp.exp(x[cl - 1:cl, :]) for x in cs_c]
    cat_rows = lambda xs: xs[0] if nc == 1 else jnp.concatenate(xs, axis=0)
    cs = cat_rows(cs_c)
    pc = cat_rows([jnp.broadcast_to(x, (cl, MIX_W)) for x in pc_c])
    e_nc = jnp.exp(-cs)
    at = -kk * jnp.exp(cs - d)
    rt = r * jnp.exp(cs)
    bt = b * e_nc
    kt = k2 * e_nc
    bh = bt * pc
    kh = kt * pc

    lane_c = lax.broadcasted_iota(jnp.int32, (cl, LANES), 1)
    first = lane_c < RW_HD
    si = lax.broadcasted_iota(jnp.int32, (rb, rb), 0)
    sj = lax.broadcasted_iota(jnp.int32, (rb, rb), 1)
    strict = si > sj
    incl = si >= sj
    eye = jnp.where(si == sj, 1.0, 0.0)

    def level_mask(sh):
        return (((si >> (sh + 1)) == (sj >> (sh + 1)))
                & (((si >> sh) & 1) == 1) & (((sj >> sh) & 1) == 0))

    n_levels = int(math.log2(cl))

    chains = [(sl, lg) for sl in rows for lg in pair_lanes]
    each = lambda fn, *lists: [fn(*xs) for xs in zip(*lists)]

    def stacked(x):
        def one(sl, lg):
            t = x[sl, lg]
            return jnp.concatenate([jnp.where(first, t, 0.0), jnp.where(first, 0.0, t)], axis=0)
        return [one(sl, lg) for sl, lg in chains]

    ats, rts, bts, kts, bhs, khs, vs = map(stacked, (at, rt, bt, kt, bh, kh, v))
    if rb == LANES:
        g = each(lambda a_, r_, b_, k_: _mm_nt(jnp.concatenate([a_, r_], axis=0),
                                               jnp.concatenate([b_, k_], axis=0)), ats, rts, bts, kts)
        g_ab, g_ak = [x[:rb, :rb] for x in g], [x[:rb, rb:] for x in g]
        g_rb, g_rk = [x[rb:, :rb] for x in g], [x[rb:, rb:] for x in g]
    else:
        g_ab, g_ak = each(_mm_nt, ats, bts), each(_mm_nt, ats, kts)
        g_rb, g_rk = each(_mm_nt, rts, bts), each(_mm_nt, rts, kts)
    a_ab = each(lambda x: jnp.where(strict, x, 0.0), g_ab)
    a_ak = each(lambda x: jnp.where(strict, x, 0.0), g_ak)
    a_rb = each(lambda x: jnp.where(incl, x, 0.0), g_rb)
    a_rk = each(lambda x: jnp.where(incl, x, 0.0), g_rk)
    t = each(lambda x: eye + jnp.where(level_mask(0), x, 0.0), a_ab)
    for sh in range(1, n_levels):
        lm = level_mask(sh)
        ta = each(lambda t_, x: _mm(t_, jnp.where(lm, x, 0.0)), t, a_ab)
        t = each(lambda t_, x: t_ + _mm(x, t_), t, ta)
    w = each(_mm, a_ak, vs)
    tw = each(lambda t_, a_, w_: _mm(t_, jnp.concatenate([a_, w_], axis=1)), t, ats, w)
    ap, vp = [x[:, :LANES] for x in tw], [x[:, LANES:] for x in tw]
    rp = each(lambda r_, a_, x: r_ + _mm(a_, x), rts, a_rb, ap)
    if rb == LANES:
        yp = each(lambda a_, c_, x, y_: _mm(jnp.concatenate([a_, c_], axis=1),
                                            jnp.concatenate([x, y_], axis=0)), a_rb, a_rk, vp, vs)
    else:
        yp = each(lambda a_, c_, x, y_: _mm(a_, x) + _mm(c_, y_), a_rb, a_rk, vp, vs)
    mn = each(_mm_tn, bhs, tw)
    diag = li == lj
    pcs = [x[:, lg] for x in pc_c for lg in pair_lanes]
    mm = each(lambda x, p: jnp.where(diag, jnp.broadcast_to(p, (LANES, LANES)), 0.0) + x[:, :LANES], mn, pcs)
    nn = each(lambda x, k_, y_: x[:, LANES:] + _mm_tn(k_, y_), mn, khs, vs)

    z = [z_scr[gi] for gi in range(RW_PAIRS)]
    y_rows = []
    for c0 in range(nc):
        base = c0 * RW_PAIRS
        y_st = [_mm(rp[base + gi], z[gi]) + yp[base + gi] for gi in range(RW_PAIRS)]
        z = [_mm(mm[base + gi], z[gi]) + nn[base + gi] for gi in range(RW_PAIRS)]
        y_rows.append(jnp.concatenate([x[:cl] + x[cl:] for x in y_st], axis=1))
    for gi in range(RW_PAIRS):
        z_scr[gi] = z[gi]
    y = cat_rows(y_rows)

    inv_hd = 1.0 / RW_HD
    yc = y - head_sum(y) * inv_hd
    var = head_sum(yc * yc) * inv_hd
    y_ref[0] = (yc * lax.rsqrt(var + GN_EPS) * lnw_ref[...] + lnb_ref[...] + bonus).astype(y_ref.dtype)

    @pl.when(c_idx == pl.num_programs(1) - 1)
    def _fin():
        for gi in range(RW_PAIRS):
            zout_ref[0, gi] = z[gi]


def rwkv_core(p, prev_row, z0, mu, w0, a0, k_k, k_a, r_k, ln_w, ln_b, wup, aup, ts, cl):
    bsz, t, _ = p.shape
    assert t % ts == 0 and ts % cl == 0
    tok = lambda off: pl.BlockSpec((1, ts, MIX_W), lambda b, c, off=off: (b, c, off))
    chan = pl.BlockSpec((1, MIX_W), lambda b, c: (0, 0))
    lora = pl.BlockSpec((LANES, MIX_W), lambda b, c: (0, 0))
    state = pl.BlockSpec((1, RW_PAIRS, LANES, LANES), lambda b, c: (b, 0, 0, 0))
    in_specs = [tok(0), tok(1), tok(2),
                pl.BlockSpec((1, ts, LANES), lambda b, c: (b, c, 3 * MIX_W // LANES)),
                pl.BlockSpec((1, 1, RW_PROJ), lambda b, c: (b, 0, 0)),
                pl.BlockSpec((1, RW_PROJ), lambda b, c: (0, 0)),
                chan, chan, chan, chan, chan, chan, chan, lora, lora, state]
    row = lambda a: a.reshape(1, -1)
    return pl.pallas_call(
        functools.partial(_rwkv_kernel, ts=ts, cl=cl),
        grid=(bsz, t // ts),
        in_specs=in_specs,
        out_specs=[pl.BlockSpec((1, ts, MIX_W), lambda b, c: (b, c, 0)), state],
        out_shape=[jax.ShapeDtypeStruct((bsz, t, MIX_W), BF16),
                   jax.ShapeDtypeStruct((bsz, RW_PAIRS, LANES, LANES), F32)],
        scratch_shapes=[pltpu.VMEM((RW_PAIRS, LANES, LANES), F32), pltpu.VMEM((1, RW_PROJ), F32)],
        compiler_params=_params(("parallel", "arbitrary")),
        name="rwkv_core",
    )(p, p, p, p, prev_row, row(mu), row(w0), row(a0), row(k_k), row(k_a), row(r_k), row(ln_w), row(ln_b),
      wup, aup, z0)


def _pair_state_in(s):
    bsz = s.shape[0]
    st = jnp.swapaxes(s.astype(F32), -1, -2).reshape(bsz, RW_PAIRS, 2, RW_HD, RW_HD)
    z = st[:, :, :, :, None, :] * jnp.eye(2, dtype=F32)[None, None, :, None, :, None]
    return z.reshape(bsz, RW_PAIRS, LANES, LANES)


def _pair_state_out(z):
    bsz = z.shape[0]
    z6 = z.reshape(bsz, RW_PAIRS, 2, RW_HD, 2, RW_HD)
    st = jnp.stack([z6[:, :, 0, :, 0, :], z6[:, :, 1, :, 1, :]], axis=2)
    return jnp.swapaxes(st.reshape(bsz, RW_HEADS, RW_HD, RW_HD), -1, -2)


def _attn_stream_width(tq):
    return min(tq, 2 * LANES)


def _diff_attn_kernel(q_ref, k_ref, v_ref, lq1, lk1, lq2, lk2, sub_ref, o_ref, vbt,
                      *, tq, tk, t_kv, t_valid, past, lam_init):
    qi = pl.program_id(2)
    kb = k_ref.at[0]

    @pl.when(qi == 0)
    def _stage():
        ei = lax.broadcasted_iota(jnp.int32, (LANES, LANES), 0)
        ej = lax.broadcasted_iota(jnp.int32, (LANES, LANES), 1)
        eye = jnp.where(ei == ej, 1.0, 0.0).astype(BF16)

        def body(c, _):
            start = pl.multiple_of(c * tk, tk)
            vt = lax.dot_general(eye, v_ref[0, pl.ds(start, tk), :], (((1,), (1,)), ((), ())),
                                 preferred_element_type=F32)
            vbt[:, pl.ds(start, tk)] = vt.astype(BF16)
            return 0

        lax.fori_loop(0, t_kv // tk, body, 0)

    lam = (jnp.exp(jnp.sum(lq1[...] * lk1[...], axis=-1, keepdims=True))
           - jnp.exp(jnp.sum(lq2[...] * lk2[...], axis=-1, keepdims=True)) + lam_init)

    lane = lax.broadcasted_iota(jnp.int32, (tq, LANES), 1)
    q = q_ref[0].astype(F32) * (DF_HD ** -0.5 * LOG2_E)
    q1 = jnp.where(lane < DF_HD, q, 0.0).astype(BF16)
    q2 = jnp.where(lane >= DF_HD, q, 0.0).astype(BF16)

    q0 = past + qi * tq
    lim_first = jnp.minimum(((q0 >> CHUNK_SHIFT) + 1) << CHUNK_SHIFT, t_valid)
    lim_last = jnp.minimum((((q0 + tq - 1) >> CHUNK_SHIFT) + 1) << CHUNK_SHIFT, t_valid)
    n_full = lim_first // tk
    n_tot = (lim_last + tk - 1) // tk

    tw = _attn_stream_width(tq)
    n_blk = tq // tw
    n_str = 2 * n_blk
    q_str = [qh[i * tw:(i + 1) * tw] for qh in (q1, q2) for i in range(n_blk)]
    q_pos = q0 + lax.broadcasted_iota(jnp.int32, (1, tq), 1)
    q_lim = jnp.minimum(((q_pos >> CHUNK_SHIFT) + 1) << CHUNK_SHIFT, t_valid)
    lim_str = [q_lim[:, i * tw:(i + 1) * tw] for _ in range(2) for i in range(n_blk)]
    rows_full = [tk] * n_str

    def tile(j, rows):
        start = pl.multiple_of(j * tk, tk)
        uniq = {n: (kb[pl.ds(start, n), :], vbt[:, pl.ds(start, n)]) for n in sorted(set(rows)) if n}
        uniq[0] = (None, None)
        return start, [uniq[n][0] for n in rows], [uniq[n][1] for n in rows]

    def qk(kcs):
        return [None if kc is None else
                lax.dot_general(kc, qs, (((1,), (1,)), ((), ())), preferred_element_type=F32)
                for kc, qs in zip(kcs, q_str)]

    def softmax_update(s, m_old, l_old, start, masked):
        live = [x is not None for x in s]
        if masked:
            k_pos = {n: start + lax.broadcasted_iota(jnp.int32, (n, tw), 0)
                     for n in set(x.shape[0] for x in s if x is not None)}
            s = [jnp.where(k_pos[x.shape[0]] < lim, x, NEG_INF) if ok else None
                 for ok, x, lim in zip(live, s, lim_str)]
        m_new = [jnp.maximum(m, jnp.max(x, axis=0, keepdims=True)) if ok else m
                 for ok, m, x in zip(live, m_old, s)]
        corr = [jnp.exp2(m - mn) if ok else None for ok, m, mn in zip(live, m_old, m_new)]
        p = [jnp.exp2(x - mn) if ok else None for ok, x, mn in zip(live, s, m_new)]
        l_new = [l * c + jnp.sum(x, axis=0, keepdims=True) if ok else l
                 for ok, l, c, x in zip(live, l_old, corr, p)]
        return m_new, corr, p, l_new

    def pv(vcts, p, acc_old, corr):
        return [a if x is None else a * c + jnp.dot(vct, x.astype(BF16), preferred_element_type=F32)
                for vct, a, c, x in zip(vcts, acc_old, corr, p)]

    def single(j, carry, masked, rows=rows_full):
        m_old, l_old, acc_old = carry
        start, kcs, vcts = tile(j, rows)
        m_new, corr, p, l_new = softmax_update(qk(kcs), m_old, l_old, start, masked)
        return tuple(m_new), tuple(l_new), tuple(pv(vcts, p, acc_old, corr))

    def pair(j2, carry):
        m_old, l_old, acc_old = carry
        start_a, kc_a, vct_a = tile(2 * j2, rows_full)
        start_b, kc_b, vct_b = tile(2 * j2 + 1, rows_full)
        s_a = qk(kc_a)
        s_b = qk(kc_b)
        m_a, corr_a, p_a, l_a = softmax_update(s_a, m_old, l_old, start_a, False)
        acc_a = pv(vct_a, p_a, acc_old, corr_a)
        m_b, corr_b, p_b, l_b = softmax_update(s_b, m_a, l_a, start_b, False)
        return tuple(m_b), tuple(l_b), tuple(pv(vct_b, p_b, acc_a, corr_b))

    carry = (tuple(jnp.full((1, tw), NEG_INF, F32) for _ in range(n_str)),
             tuple(jnp.zeros((1, tw), F32) for _ in range(n_str)),
             tuple(jnp.zeros((LANES, tw), F32) for _ in range(n_str)))
    n_pair = n_full // 2
    carry = lax.fori_loop(0, n_pair, pair, carry)
    carry = lax.fori_loop(2 * n_pair, n_full, functools.partial(single, masked=False), carry)
    if past == 0 and tq % tk == 0 and t_valid == t_kv:
        for dt in range(tq // tk):
            rows = [min(max((i + 1) * tw - dt * tk, 0), tk) for _ in range(2) for i in range(n_blk)]
            carry = single(n_full + dt, carry, True, rows)
    else:
        carry = lax.fori_loop(n_full, n_tot, functools.partial(single, masked=True), carry)
    _, l_fin, acc_fin = carry
    cat = lambda xs: xs[0] if len(xs) == 1 else jnp.concatenate(xs, axis=1)
    l1, l2 = cat(l_fin[:n_blk]), cat(l_fin[n_blk:])
    acc1, acc2 = cat(acc_fin[:n_blk]), cat(acc_fin[n_blk:])
    o_t = acc1 * (1.0 / l1) - acc2 * (lam / l2)
    ms = jnp.mean(o_t * o_t, axis=0, keepdims=True)
    o_t = o_t * lax.rsqrt(ms + EPS)
    o_ref[0] = (o_t.T * sub_ref[...] * (1.0 - lam_init)).astype(o_ref.dtype)


def diff_attn(q, k_all, v_all, past, t_valid, lq1, lk1, lq2, lk2, subln, lam_init, tq, tk):
    bsz, t_q, _ = q.shape
    t_kv = k_all.shape[1]
    assert t_q % tq == 0 and t_kv % tk == 0 and t_kv % LANES == 0 and tq % LANES == 0
    kern = functools.partial(_diff_attn_kernel, tq=tq, tk=tk, t_kv=t_kv, t_valid=t_valid,
                             past=past, lam_init=lam_init)
    vec = pl.BlockSpec((1, DF_HD), lambda b, h, i: (0, 0))
    kv = pl.BlockSpec((1, t_kv, LANES), lambda b, h, i: (b, 0, h))
    row = lambda a: a.reshape(1, -1)
    return pl.pallas_call(
        kern,
        grid=(bsz, DF_HEADS, t_q // tq),
        in_specs=[pl.BlockSpec((1, tq, LANES), lambda b, h, i: (b, i, h)), kv, kv,
                  vec, vec, vec, vec, pl.BlockSpec((1, DF_VD), lambda b, h, i: (0, 0))],
        out_specs=pl.BlockSpec((1, tq, LANES), lambda b, h, i: (b, i, h)),
        out_shape=jax.ShapeDtypeStruct((bsz, t_q, MIX_W), BF16),
        scratch_shapes=[pltpu.VMEM((LANES, t_kv), BF16)],
        compiler_params=_params(("parallel", "parallel", "arbitrary")),
        name="diff_attn",
    )(q, k_all, v_all, row(lq1), row(lk1), row(lq2), row(lk2), row(subln))


def _gate_out_kernel(x_ref, ymix_ref, gq_ref, mk_ref, mv_ref, wout_ref, nw_ref, *out_refs):
    gq = gq_ref[0]
    gate = gq[:, :BRANCH].astype(F32)
    heads = []
    for h in range(MEM_HEADS):
        sl = slice(h * MEM_HD, (h + 1) * MEM_HD)
        qh = gq[:, BRANCH + h * MEM_HD:BRANCH + (h + 1) * MEM_HD]
        s = _mm_nt(qh, mk_ref[0][:, sl]) * (MEM_HD ** -0.5)
        e = jnp.exp(s - jnp.max(s, axis=-1, keepdims=True))
        prob = e * (1.0 / jnp.sum(e, axis=-1, keepdims=True))
        heads.append(_mm(prob, mv_ref[0][:, sl]))
    branch = jnp.concatenate([ymix_ref[0].astype(F32)] + heads, axis=-1)
    act = (branch * (gate * _sigmoid(gate))).astype(BF16)
    xn = x_ref[0] + jnp.dot(act, wout_ref[...], preferred_element_type=F32)
    normed_ref = out_refs[-1]
    normed_ref[0] = (_rms(xn) * nw_ref[...]).astype(normed_ref.dtype)
    if len(out_refs) == 2:
        out_refs[0][0] = xn


def gate_out(x, ymix, gq, mk, mv, w_out, next_norm_w, last):
    bsz, t, d = x.shape
    bm = min(GATE_ROWS, t)
    assert t % bm == 0
    tokb = lambda w: pl.BlockSpec((1, bm, w), lambda b, i: (b, i, 0))
    mem = pl.BlockSpec((1, N_MEM, MEM_W), lambda b, i: (b, 0, 0))
    if last:
        out_specs, out_shape = tokb(d), jax.ShapeDtypeStruct((bsz, t, d), F32)
    else:
        out_specs = [tokb(d), tokb(d)]
        out_shape = [jax.ShapeDtypeStruct((bsz, t, d), F32), jax.ShapeDtypeStruct((bsz, t, d), BF16)]
    return pl.pallas_call(
        _gate_out_kernel,
        grid=(bsz, t // bm),
        in_specs=[tokb(d), tokb(MIX_W), tokb(BRANCH + MEM_W), mem, mem,
                  pl.BlockSpec((BRANCH, d), lambda b, i: (0, 0)),
                  pl.BlockSpec((1, d), lambda b, i: (0, 0))],
        out_specs=out_specs,
        out_shape=out_shape,
        compiler_params=_params(("parallel", "parallel")),
        name="gate_out",
    )(x, ymix, gq, mk, mv, w_out, next_norm_w.reshape(1, d))


def _mem_kv(mem, norm_w, wk, wv):
    bsz = mem.shape[0]
    hm = rmsnorm_bf16(mem.reshape(bsz * N_MEM, D_MODEL), norm_w)
    mk = proj(hm, wk.astype(BF16), MEM_W, F32)
    mv = proj(hm, wv.astype(BF16), MEM_W, F32)
    return mk.reshape(bsz, N_MEM, MEM_W), mv.reshape(bsz, N_MEM, MEM_W)


def _rwkv_layer(x, h, mk, mv, prev_row, s0, w_out, next_norm_w, w_in, mu, w0, w_up, a0, a_up,
                k_k, k_a, r_k, ln_w, ln_b, ts, cl):
    bsz, t, d = x.shape
    w_p = jnp.pad(w_in[:, :RW_PROJ], ((0, 0), (0, RW_PROJ_PAD - RW_PROJ))).astype(BF16)
    w_gq = jnp.concatenate([w_in[:, RW_PROJ + MEM_W:], w_in[:, RW_PROJ:RW_PROJ + MEM_W]], axis=1).astype(BF16)
    p = proj(h, w_p, RW_PROJ_PAD // 2, BF16).reshape(bsz, t, RW_PROJ_PAD)
    gq = proj(h, w_gq, (BRANCH + MEM_W) // 2, BF16).reshape(bsz, t, BRANCH + MEM_W)
    p_last = proj(h.reshape(bsz, t, d)[:, t - 1], w_p, RW_PROJ_PAD // 2, F32)[:, :RW_PROJ]
    zero = jnp.zeros((LORA, MIX_W), F32)
    wup = jnp.concatenate([w_up, zero], axis=0).astype(BF16)
    aup = jnp.concatenate([zero, a_up], axis=0).astype(BF16)
    y_mix, z_t = rwkv_core(p, prev_row.reshape(bsz, 1, RW_PROJ), _pair_state_in(s0), mu, w0, a0, k_k, k_a,
                           r_k.reshape(-1), ln_w, ln_b, wup, aup, ts, cl)
    x_new, h_new = gate_out(x, y_mix, gq, mk, mv, w_out.astype(BF16), next_norm_w, last=False)
    return x_new, h_new.reshape(bsz * t, d), _pair_state_out(z_t), p_last


def _round_up(n, m):
    return (n + m - 1) // m * m


def _diff_layer(x, h, mk, mv, k_past, v_past, layer_idx, w_out, final_w, w_in, lq1, lk1, lq2, lk2,
                subln, tq, tk):
    bsz, t, d = x.shape
    cols = lambda lo, hi: w_in[:, lo:hi].astype(BF16)
    q = proj(h, cols(0, MIX_W), MIX_W, BF16).reshape(bsz, t, MIX_W)
    k4, k = proj_heads(h, cols(MIX_W, 2 * MIX_W))
    v4, v = proj_heads(h, cols(2 * MIX_W, 3 * MIX_W))
    k, v = k.reshape(bsz, t, MIX_W), v.reshape(bsz, t, MIX_W)
    w_gq = jnp.concatenate([w_in[:, 3 * MIX_W + MEM_W:], w_in[:, 3 * MIX_W:3 * MIX_W + MEM_W]], axis=1).astype(BF16)
    gq = proj(h, w_gq, (BRANCH + MEM_W) // 2, BF16).reshape(bsz, t, BRANCH + MEM_W)
    past = 0 if k_past is None else k_past.shape[1]
    t_valid = past + t
    t_q, t_kv = _round_up(t, tq), _round_up(t_valid, tk)
    pad_rows = lambda a, n: a if n == 0 else jnp.pad(a, ((0, 0), (0, n), (0, 0)))
    k_all, v_all = k, v
    if k_past is not None:
        k_all = jnp.concatenate([k_past.reshape(bsz, past, MIX_W).astype(BF16), k], axis=1)
        v_all = jnp.concatenate([v_past.reshape(bsz, past, MIX_W).astype(BF16), v], axis=1)
    lam_init = 0.8 - 0.6 * math.exp(-0.3 * layer_idx)
    o = diff_attn(pad_rows(q, t_q - t), pad_rows(k_all, t_kv - t_valid), pad_rows(v_all, t_kv - t_valid),
                  past, t_valid, lq1, lk1, lq2, lk2, subln, lam_init, tq, tk)[:, :t]
    y = gate_out(x, o, gq, mk, mv, w_out.astype(BF16), final_w, last=True)
    return y, k4.reshape(bsz, t, DF_HEADS, 2 * DF_HD), v4.reshape(bsz, t, DF_HEADS, DF_VD)


def kernel(x_prompt, mem_prompt, x_sample, state_rwkv, state_shift, cache_k, cache_v, cache_mem_k, cache_mem_v, norm_w, mem_norm_w, w_mem_k, w_mem_v, w_out, final_norm_w, rw_in, rw_mu, rw_w0, rw_w_up, rw_a0, rw_a_up, rw_k_k, rw_k_a, rw_r_k, rw_ln_w, rw_ln_b, df_in, df_lq1, df_lk1, df_lq2, df_lk2, df_subln):
    bp, tp, d = x_prompt.shape
    bs, tsm, _ = x_sample.shape
    mem_s = lambda m: m.reshape(bs, N_MEM, MEM_W)
    mk0, mv0 = _mem_kv(mem_prompt, mem_norm_w[0], w_mem_k[0], w_mem_v[0])
    mk1, mv1 = _mem_kv(mem_prompt, mem_norm_w[1], w_mem_k[1], w_mem_v[1])

    rw = (rw_in[0], rw_mu[0], rw_w0[0], rw_w_up[0], rw_a0[0], rw_a_up[0], rw_k_k[0], rw_k_a[0], rw_r_k[0],
          rw_ln_w[0], rw_ln_b[0])
    hp = rmsnorm_bf16(x_prompt.reshape(bp * tp, d), norm_w[0])
    xp, hp, p_s, p_shift = _rwkv_layer(
        x_prompt, hp, mk0, mv0, jnp.zeros((bp, RW_PROJ), F32), jnp.zeros((bp, RW_HEADS, RW_HD, RW_HD), F32),
        w_out[0], norm_w[1], *rw, ts=RW_STEP, cl=RW_CHUNK)
    hs = rmsnorm_bf16(x_sample.reshape(bs * tsm, d), norm_w[0])
    xs, hs, s_s, s_shift = _rwkv_layer(
        x_sample, hs, mem_s(cache_mem_k[0]), mem_s(cache_mem_v[0]), state_shift[0], state_rwkv[0],
        w_out[0], norm_w[1], *rw, ts=tsm, cl=tsm)

    df = (df_in[0], df_lq1[0], df_lk1[0], df_lq2[0], df_lk2[0], df_subln[0])
    y_prompt, p_k, p_v = _diff_layer(xp, hp, mk1, mv1, None, None, 1, w_out[1], final_norm_w, *df,
                                     tq=ATTN_Q_TILE, tk=ATTN_K_TILE)
    t_kv_s = _round_up(cache_k.shape[2] + tsm, LANES)
    y_sample, s_k, s_v = _diff_layer(xs, hs, mem_s(cache_mem_k[1]), mem_s(cache_mem_v[1]), cache_k[0], cache_v[0], 1,
                                     w_out[1], final_norm_w, *df, tq=LANES, tk=t_kv_s)

    mem4 = lambda m: m.reshape(bp, N_MEM, MEM_HEADS, MEM_HD)
    return (y_prompt, y_sample, p_s[None], p_shift[None], p_k[None], p_v[None],
            jnp.stack([mem4(mk0), mem4(mk1)]), jnp.stack([mem4(mv0), mem4(mv1)]),
            s_s[None], s_shift[None], s_k[None], s_v[None])
```

```python
import functools
import math

import jax
import jax.numpy as jnp
from jax import lax
from jax.experimental import pallas as pl
from jax.experimental.pallas import tpu as pltpu

D_MODEL = 1024
CHUNK = 64
CHUNK_SHIFT = 6
N_MEM = 256
BRANCH = 2 * D_MODEL
MIX_W = 3 * D_MODEL // 2
MEM_W = D_MODEL // 2
MEM_HEADS = 4
MEM_HD = MEM_W // MEM_HEADS
RW_HD = 64
RW_HEADS = MIX_W // RW_HD
LORA = 64
RW_PROJ = 3 * MIX_W + 2 * LORA
DF_HD = 64
DF_HEADS = MIX_W // (2 * DF_HD)
DF_VD = 2 * DF_HD
EPS = 1e-6
GN_EPS = 64e-5
NEG_INF = -1e30
LOG2_E = 1.4426950408889634

LANES = 128
RW_PAIRS = RW_HEADS // 2
RW_PROJ_PAD = 4864
V7X_VMEM_LIMIT = 56 * 1024 * 1024

ROW_BLOCK = 1024
GATE_ROWS = 512
RW_CHUNK = 64
RW_STEP = 2 * RW_CHUNK
ATTN_Q_TILE = 1024
ATTN_K_TILE = 512

BF16 = jnp.bfloat16
F32 = jnp.float32


def _mm(a, b):
    return jnp.dot(a.astype(BF16), b.astype(BF16), preferred_element_type=F32)


def _mm_nt(a, b):
    return lax.dot_general(a.astype(BF16), b.astype(BF16), (((1,), (1,)), ((), ())),
                           preferred_element_type=F32)


def _mm_tn(a, b):
    return lax.dot_general(a.astype(BF16), b.astype(BF16), (((0,), (0,)), ((), ())),
                           preferred_element_type=F32)


def _sigmoid(x):
    return 0.5 * jnp.tanh(0.5 * x) + 0.5


def _softplus(x):
    return jnp.maximum(x, 0.0) + jnp.log(1.0 + jnp.exp(-jnp.abs(x)))


def _rms(x):
    return x * lax.rsqrt(jnp.mean(x * x, axis=-1, keepdims=True) + EPS)


def _params(sem):
    return pltpu.CompilerParams(dimension_semantics=sem, vmem_limit_bytes=V7X_VMEM_LIMIT)


def _rmsnorm_kernel(x_ref, nw_ref, o_ref):
    o_ref[...] = (_rms(x_ref[...]) * nw_ref[...]).astype(o_ref.dtype)


def rmsnorm_bf16(x, norm_w):
    n, d = x.shape
    bm = min(ROW_BLOCK, n)
    assert n % bm == 0
    return pl.pallas_call(
        _rmsnorm_kernel,
        grid=(n // bm,),
        in_specs=[pl.BlockSpec((bm, d), lambda i: (i, 0)), pl.BlockSpec((1, d), lambda i: (0, 0))],
        out_specs=pl.BlockSpec((bm, d), lambda i: (i, 0)),
        out_shape=jax.ShapeDtypeStruct((n, d), BF16),
        compiler_params=_params(("parallel",)),
        name="rmsnorm",
    )(x, norm_w.reshape(1, d))


def _proj_kernel(h_ref, w_ref, o_ref):
    o_ref[...] = jnp.dot(h_ref[...], w_ref[...], preferred_element_type=F32).astype(o_ref.dtype)


def proj(h, w, bn, out_dtype):
    n, d = h.shape
    nout = w.shape[1]
    bm = min(ROW_BLOCK, n)
    assert n % bm == 0 and nout % bn == 0
    return pl.pallas_call(
        _proj_kernel,
        grid=(nout // bn, n // bm),
        in_specs=[pl.BlockSpec((bm, d), lambda j, i: (i, 0)),
                  pl.BlockSpec((d, bn), lambda j, i: (0, j))],
        out_specs=pl.BlockSpec((bm, bn), lambda j, i: (i, j)),
        out_shape=jax.ShapeDtypeStruct((n, nout), out_dtype),
        compiler_params=_params(("parallel", "parallel")),
        name="proj",
    )(h, w)


def _proj_heads_kernel(h_ref, w_ref, o4_ref, ob_ref):
    res = jnp.dot(h_ref[...], w_ref[...], preferred_element_type=F32)
    ob_ref[...] = res.astype(BF16)
    for hh in range(DF_HEADS):
        o4_ref[:, hh, :] = res[:, hh * LANES:(hh + 1) * LANES]


def proj_heads(h, w):
    n, d = h.shape
    bm = min(ROW_BLOCK, n)
    assert n % bm == 0 and w.shape[1] == MIX_W
    return pl.pallas_call(
        _proj_heads_kernel,
        grid=(n // bm,),
        in_specs=[pl.BlockSpec((bm, d), lambda i: (i, 0)), pl.BlockSpec((d, MIX_W), lambda i: (0, 0))],
        out_specs=[pl.BlockSpec((bm, DF_HEADS, LANES), lambda i: (i, 0, 0)),
                   pl.BlockSpec((bm, MIX_W), lambda i: (i, 0))],
        out_shape=[jax.ShapeDtypeStruct((n, DF_HEADS, LANES), F32), jax.ShapeDtypeStruct((n, MIX_W), BF16)],
        compiler_params=_params(("parallel",)),
        name="proj_heads",
    )(h, w)


def _split3(x):
    hi = x.astype(BF16)
    r1 = x - hi.astype(F32)
    mid = r1.astype(BF16)
    lo = (r1 - mid.astype(F32)).astype(BF16)
    return hi, mid, lo


def _rwkv_kernel(r_ref, k_ref, v_ref, pp_ref, prev_ref, mu_ref, w0_ref, a0_ref, kk_ref, ka_ref, rk_ref,
                 lnw_ref, lnb_ref, wup_ref, aup_ref, z0_ref,
                 y_ref, zout_ref,
                 z_scr, carry, *, ts, cl):
    nc = ts // cl
    rb = 2 * cl
    c_idx = pl.program_id(1)
    k_off, v_off, pp_off = MIX_W, 2 * MIX_W, 3 * MIX_W

    @pl.when(c_idx == 0)
    def _init():
        z_scr[...] = z0_ref[0]
        carry[...] = prev_ref[0]

    def shifted(x_ref, lo):
        x = x_ref[0].astype(F32)
        hi = lo + x.shape[1]
        row = lax.broadcasted_iota(jnp.int32, x.shape, 0)
        prev = jnp.where(row == 0, carry[:, lo:hi], pltpu.roll(x, 1, 0))
        carry[:, lo:hi] = x[ts - 1:ts, :]
        return x + mu_ref[:, lo:hi] * (prev - x)

    r = shifted(r_ref, 0)
    k = shifted(k_ref, k_off)
    v = shifted(v_ref, v_off)
    pp = shifted(pp_ref, pp_off)

    lw = w0_ref[...] + _mm(jnp.tanh(pp), wup_ref[...])
    log_w = -_softplus(-lw) - 0.5
    d = -jnp.exp(log_w)
    a = _sigmoid(a0_ref[...] + _mm(pp, aup_ref[...]))

    li = lax.broadcasted_iota(jnp.int32, (LANES, LANES), 0)
    lj = lax.broadcasted_iota(jnp.int32, (LANES, LANES), 1)
    seg = jnp.where((li >> 6) == (lj >> 6), 1.0, 0.0).astype(BF16)
    pair_lanes = [slice(g * LANES, (g + 1) * LANES) for g in range(RW_PAIRS)]

    def head_sum(x):
        return jnp.concatenate([_mm(x[:, lg], seg) for lg in pair_lanes], axis=1)

    kk = k * kk_ref[...]
    kk = kk * lax.rsqrt(jnp.maximum(head_sum(kk * kk), 1e-24))
    k2 = k * (1.0 + (a - 1.0) * ka_ref[...])
    b = kk * a
    bonus = head_sum(r * k2 * rk_ref[...]) * v

    ci = lax.broadcasted_iota(jnp.int32, (cl, cl), 0)
    cj = lax.broadcasted_iota(jnp.int32, (cl, cl), 1)
    tril = jnp.where(ci >= cj, 1.0, 0.0).astype(BF16)
    rows = [slice(c0 * cl, (c0 + 1) * cl) for c0 in range(nc)]
    parts = _split3(d)
    cs_c = [sum(jnp.dot(tril, part[sl], preferred_element_type=F32) for part in parts) for sl in rows]
    pc_c = [jnp.exp(x[cl - 1:cl, :]) for x in cs_c]
    cat_rows = lambda xs: xs[0] if nc == 1 else jnp.concatenate(xs, axis=0)
    cs = cat_rows(cs_c)
    pc = cat_rows([jnp.broadcast_to(x, (cl, MIX_W)) for x in pc_c])
    e_nc = jnp.exp(-cs)
    at = -kk * jnp.exp(cs - d)
    rt = r * jnp.exp(cs)
    bt = b * e_nc
    kt = k2 * e_nc
    bh = bt * pc
    kh = kt * pc

    lane_c = lax.broadcasted_iota(jnp.int32, (cl, LANES), 1)
    first = lane_c < RW_HD
    si = lax.broadcasted_iota(jnp.int32, (rb, rb), 0)
    sj = lax.broadcasted_iota(jnp.int32, (rb, rb), 1)
    strict = si > sj
    incl = si >= sj
    eye = jnp.where(si == sj, 1.0, 0.0)

    def level_mask(sh):
        return (((si >> (sh + 1)) == (sj >> (sh + 1)))
                & (((si >> sh) & 1) == 1) & (((sj >> sh) & 1) == 0))

    n_levels = int(math.log2(cl))

    chains = [(sl, lg) for sl in rows for lg in pair_lanes]
    each = lambda fn, *lists: [fn(*xs) for xs in zip(*lists)]

    def stacked(x):
        def one(sl, lg):
            t = x[sl, lg]
            return jnp.concatenate([jnp.where(first, t, 0.0), jnp.where(first, 0.0, t)], axis=0)
        return [one(sl, lg) for sl, lg in chains]

    ats, rts, bts, kts, bhs, khs, vs = map(stacked, (at, rt, bt, kt, bh, kh, v))
    if rb == LANES:
        g = each(lambda a_, r_, b_, k_: _mm_nt(jnp.concatenate([a_, r_], axis=0),
                                               jnp.concatenate([b_, k_], axis=0)), ats, rts, bts, kts)
        g_ab, g_ak = [x[:rb, :rb] for x in g], [x[:rb, rb:] for x in g]
        g_rb, g_rk = [x[rb:, :rb] for x in g], [x[rb:, rb:] for x in g]
    else:
        g_ab, g_ak = each(_mm_nt, ats, bts), each(_mm_nt, ats, kts)
        g_rb, g_rk = each(_mm_nt, rts, bts), each(_mm_nt, rts, kts)
    a_ab = each(lambda x: jnp.where(strict, x, 0.0), g_ab)
    a_ak = each(lambda x: jnp.where(strict, x, 0.0), g_ak)
    a_rb = each(lambda x: jnp.where(incl, x, 0.0), g_rb)
    a_rk = each(lambda x: jnp.where(incl, x, 0.0), g_rk)
    t = each(lambda x: eye + jnp.where(level_mask(0), x, 0.0), a_ab)
    for sh in range(1, n_levels):
        lm = level_mask(sh)
        ta = each(lambda t_, x: _mm(t_, jnp.where(lm, x, 0.0)), t, a_ab)
        t = each(lambda t_, x: t_ + _mm(x, t_), t, ta)
    w = each(_mm, a_ak, vs)
    tw = each(lambda t_, a_, w_: _mm(t_, jnp.concatenate([a_, w_], axis=1)), t, ats, w)
    ap, vp = [x[:, :LANES] for x in tw], [x[:, LANES:] for x in tw]
    rp = each(lambda r_, a_, x: r_ + _mm(a_, x), rts, a_rb, ap)
    if rb == LANES:
        yp = each(lambda a_, c_, x, y_: _mm(jnp.concatenate([a_, c_], axis=1),
                                            jnp.concatenate([x, y_], axis=0)), a_rb, a_rk, vp, vs)
    else:
        yp = each(lambda a_, c_, x, y_: _mm(a_, x) + _mm(c_, y_), a_rb, a_rk, vp, vs)
    mn = each(_mm_tn, bhs, tw)
    diag = li == lj
    pcs = [x[:, lg] for x in pc_c for lg in pair_lanes]
    mm = each(lambda x, p: jnp.where(diag, jnp.broadcast_to(p, (LANES, LANES)), 0.0) + x[:, :LANES], mn, pcs)
    nn = each(lambda x, k_, y_: x[:, LANES:] + _mm_tn(k_, y_), mn, khs, vs)

    z = [z_scr[gi] for gi in range(RW_PAIRS)]
    y_rows = []
    for c0 in range(nc):
        base = c0 * RW_PAIRS
        y_st = [_mm(rp[base + gi], z[gi]) + yp[base + gi] for gi in range(RW_PAIRS)]
        z = [_mm(mm[base + gi], z[gi]) + nn[base + gi] for gi in range(RW_PAIRS)]
        y_rows.append(jnp.concatenate([x[:cl] + x[cl:] for x in y_st], axis=1))
    for gi in range(RW_PAIRS):
        z_scr[gi] = z[gi]
    y = cat_rows(y_rows)

    inv_hd = 1.0 / RW_HD
    yc = y - head_sum(y) * inv_hd
    var = head_sum(yc * yc) * inv_hd
    y_ref[0] = (yc * lax.rsqrt(var + GN_EPS) * lnw_ref[...] + lnb_ref[...] + bonus).astype(y_ref.dtype)

    @pl.when(c_idx == pl.num_programs(1) - 1)
    def _fin():
        for gi in range(RW_PAIRS):
            zout_ref[0, gi] = z[gi]


def rwkv_core(p, prev_row, z0, mu, w0, a0, k_k, k_a, r_k, ln_w, ln_b, wup, aup, ts, cl):
    bsz, t, _ = p.shape
    assert t % ts == 0 and ts % cl == 0
    tok = lambda off: pl.BlockSpec((1, ts, MIX_W), lambda b, c, off=off: (b, c, off))
    chan = pl.BlockSpec((1, MIX_W), lambda b, c: (0, 0))
    lora = pl.BlockSpec((LANES, MIX_W), lambda b, c: (0, 0))
    state = pl.BlockSpec((1, RW_PAIRS, LANES, LANES), lambda b, c: (b, 0, 0, 0))
    in_specs = [tok(0), tok(1), tok(2),
                pl.BlockSpec((1, ts, LANES), lambda b, c: (b, c, 3 * MIX_W // LANES)),
                pl.BlockSpec((1, 1, RW_PROJ), lambda b, c: (b, 0, 0)),
                pl.BlockSpec((1, RW_PROJ), lambda b, c: (0, 0)),
                chan, chan, chan, chan, chan, chan, chan, lora, lora, state]
    row = lambda a: a.reshape(1, -1)
    return pl.pallas_call(
        functools.partial(_rwkv_kernel, ts=ts, cl=cl),
        grid=(bsz, t // ts),
        in_specs=in_specs,
        out_specs=[pl.BlockSpec((1, ts, MIX_W), lambda b, c: (b, c, 0)), state],
        out_shape=[jax.ShapeDtypeStruct((bsz, t, MIX_W), BF16),
                   jax.ShapeDtypeStruct((bsz, RW_PAIRS, LANES, LANES), F32)],
        scratch_shapes=[pltpu.VMEM((RW_PAIRS, LANES, LANES), F32), pltpu.VMEM((1, RW_PROJ), F32)],
        compiler_params=_params(("parallel", "arbitrary")),
        name="rwkv_core",
    )(p, p, p, p, prev_row, row(mu), row(w0), row(a0), row(k_k), row(k_a), row(r_k), row(ln_w), row(ln_b),
      wup, aup, z0)


def _pair_state_in(s):
    bsz = s.shape[0]
    st = jnp.swapaxes(s.astype(F32), -1, -2).reshape(bsz, RW_PAIRS, 2, RW_HD, RW_HD)
    z = st[:, :, :, :, None, :] * jnp.eye(2, dtype=F32)[None, None, :, None, :, None]
    return z.reshape(bsz, RW_PAIRS, LANES, LANES)


def _pair_state_out(z):
    bsz = z.shape[0]
    z6 = z.reshape(bsz, RW_PAIRS, 2, RW_HD, 2, RW_HD)
    st = jnp.stack([z6[:, :, 0, :, 0, :], z6[:, :, 1, :, 1, :]], axis=2)
    return jnp.swapaxes(st.reshape(bsz, RW_HEADS, RW_HD, RW_HD), -1, -2)


SUM_ROWS = 16


def _attn_stream_width(tq):
    return min(tq, 2 * LANES)


def _diff_attn_kernel(q_ref, k_ref, v_ref, lq1, lk1, lq2, lk2, sub_ref, o_ref, vbt,
                      *, tq, tk, t_kv, t_valid, past, lam_init):
    qi = pl.program_id(2)
    kb = k_ref.at[0]

    @pl.when(qi == 0)
    def _stage():
        ei = lax.broadcasted_iota(jnp.int32, (LANES, LANES), 0)
        ej = lax.broadcasted_iota(jnp.int32, (LANES, LANES), 1)
        eye = jnp.where(ei == ej, 1.0, 0.0).astype(BF16)

        def body(c, _):
            start = pl.multiple_of(c * tk, tk)
            vt = lax.dot_general(eye, v_ref[0, pl.ds(start, tk), :], (((1,), (1,)), ((), ())),
                                 preferred_element_type=F32)
            vbt[:LANES, pl.ds(start, tk)] = vt.astype(BF16)
            return 0

        lax.fori_loop(0, t_kv // tk, body, 0)
        ones_row = lax.broadcasted_iota(jnp.int32, (SUM_ROWS, t_kv), 0) == 0
        vbt[LANES:, :] = jnp.where(ones_row, 1.0, 0.0).astype(BF16)

    lam = (jnp.exp(jnp.sum(lq1[...] * lk1[...], axis=-1, keepdims=True))
           - jnp.exp(jnp.sum(lq2[...] * lk2[...], axis=-1, keepdims=True)) + lam_init)

    lane = lax.broadcasted_iota(jnp.int32, (tq, LANES), 1)
    q = q_ref[0].astype(F32) * (DF_HD ** -0.5 * LOG2_E)
    q1 = jnp.where(lane < DF_HD, q, 0.0).astype(BF16)
    q2 = jnp.where(lane >= DF_HD, q, 0.0).astype(BF16)

    q0 = past + qi * tq
    lim_first = jnp.minimum(((q0 >> CHUNK_SHIFT) + 1) << CHUNK_SHIFT, t_valid)
    lim_last = jnp.minimum((((q0 + tq - 1) >> CHUNK_SHIFT) + 1) << CHUNK_SHIFT, t_valid)
    n_full = lim_first // tk
    n_tot = (lim_last + tk - 1) // tk

    tw = _attn_stream_width(tq)
    n_blk = tq // tw
    n_str = 2 * n_blk
    q_str = [qh[i * tw:(i + 1) * tw] for qh in (q1, q2) for i in range(n_blk)]
    q_pos = q0 + lax.broadcasted_iota(jnp.int32, (1, tq), 1)
    q_lim = jnp.minimum(((q_pos >> CHUNK_SHIFT) + 1) << CHUNK_SHIFT, t_valid)
    lim_str = [q_lim[:, i * tw:(i + 1) * tw] for _ in range(2) for i in range(n_blk)]
    rows_full = [tk] * n_str

    def tile(j, rows):
        start = pl.multiple_of(j * tk, tk)
        uniq = {n: (kb[pl.ds(start, n), :], vbt[:, pl.ds(start, n)]) for n in sorted(set(rows)) if n}
        uniq[0] = (None, None)
        return start, [uniq[n][0] for n in rows], [uniq[n][1] for n in rows]

    def qk(kcs):
        return [None if kc is None else
                lax.dot_general(kc, qs, (((1,), (1,)), ((), ())), preferred_element_type=F32)
                for kc, qs in zip(kcs, q_str)]

    def softmax_update(s, m_old, start, masked):
        live = [x is not None for x in s]
        if masked:
            k_pos = {n: start + lax.broadcasted_iota(jnp.int32, (n, tw), 0)
                     for n in set(x.shape[0] for x in s if x is not None)}
            s = [jnp.where(k_pos[x.shape[0]] < lim, x, NEG_INF) if ok else None
                 for ok, x, lim in zip(live, s, lim_str)]
        m_new = [jnp.maximum(m, jnp.max(x, axis=0, keepdims=True)) if ok else m
                 for ok, m, x in zip(live, m_old, s)]
        corr = [jnp.exp2(m - mn) if ok else None for ok, m, mn in zip(live, m_old, m_new)]
        p = [jnp.exp2(x - mn) if ok else None for ok, x, mn in zip(live, s, m_new)]
        return m_new, corr, p

    def pv(vcts, p, acc_old, corr):
        return [a if x is None else a * c + jnp.dot(vct, x.astype(BF16), preferred_element_type=F32)
                for vct, a, c, x in zip(vcts, acc_old, corr, p)]

    def single(j, carry, masked, rows=rows_full):
        m_old, acc_old = carry
        start, kcs, vcts = tile(j, rows)
        m_new, corr, p = softmax_update(qk(kcs), m_old, start, masked)
        return tuple(m_new), tuple(pv(vcts, p, acc_old, corr))

    def pair(j2, carry):
        m_old, acc_old = carry
        start_a, kc_a, vct_a = tile(2 * j2, rows_full)
        start_b, kc_b, vct_b = tile(2 * j2 + 1, rows_full)
        s_a = qk(kc_a)
        s_b = qk(kc_b)
        m_a, corr_a, p_a = softmax_update(s_a, m_old, start_a, False)
        acc_a = pv(vct_a, p_a, acc_old, corr_a)
        m_b, corr_b, p_b = softmax_update(s_b, m_a, start_b, False)
        return tuple(m_b), tuple(pv(vct_b, p_b, acc_a, corr_b))

    carry = (tuple(jnp.full((1, tw), NEG_INF, F32) for _ in range(n_str)),
             tuple(jnp.zeros((LANES + SUM_ROWS, tw), F32) for _ in range(n_str)))
    n_pair = n_full // 2
    carry = lax.fori_loop(0, n_pair, pair, carry)
    carry = lax.fori_loop(2 * n_pair, n_full, functools.partial(single, masked=False), carry)
    if past == 0 and tq % tk == 0 and t_valid == t_kv:
        for dt in range(tq // tk):
            rows = [min(max((i + 1) * tw - dt * tk, 0), tk) for _ in range(2) for i in range(n_blk)]
            carry = single(n_full + dt, carry, True, rows)
    else:
        carry = lax.fori_loop(n_full, n_tot, functools.partial(single, masked=True), carry)
    _, acc_fin = carry
    cat = lambda xs: xs[0] if len(xs) == 1 else jnp.concatenate(xs, axis=1)
    l1 = cat([x[LANES:LANES + 1] for x in acc_fin[:n_blk]])
    l2 = cat([x[LANES:LANES + 1] for x in acc_fin[n_blk:]])
    acc1 = cat([x[:LANES] for x in acc_fin[:n_blk]])
    acc2 = cat([x[:LANES] for x in acc_fin[n_blk:]])
    o_t = acc1 * (1.0 / l1) - acc2 * (lam / l2)
    ms = jnp.mean(o_t * o_t, axis=0, keepdims=True)
    o_t = o_t * lax.rsqrt(ms + EPS)
    o_ref[0] = (o_t.T * sub_ref[...] * (1.0 - lam_init)).astype(o_ref.dtype)


def diff_attn(q, k_all, v_all, past, t_valid, lq1, lk1, lq2, lk2, subln, lam_init, tq, tk):
    bsz, t_q, _ = q.shape
    t_kv = k_all.shape[1]
    assert t_q % tq == 0 and t_kv % tk == 0 and t_kv % LANES == 0 and tq % LANES == 0
    kern = functools.partial(_diff_attn_kernel, tq=tq, tk=tk, t_kv=t_kv, t_valid=t_valid,
                             past=past, lam_init=lam_init)
    vec = pl.BlockSpec((1, DF_HD), lambda b, h, i: (0, 0))
    kv = pl.BlockSpec((1, t_kv, LANES), lambda b, h, i: (b, 0, h))
    row = lambda a: a.reshape(1, -1)
    return pl.pallas_call(
        kern,
        grid=(bsz, DF_HEADS, t_q // tq),
        in_specs=[pl.BlockSpec((1, tq, LANES), lambda b, h, i: (b, i, h)), kv, kv,
                  vec, vec, vec, vec, pl.BlockSpec((1, DF_VD), lambda b, h, i: (0, 0))],
        out_specs=pl.BlockSpec((1, tq, LANES), lambda b, h, i: (b, i, h)),
        out_shape=jax.ShapeDtypeStruct((bsz, t_q, MIX_W), BF16),
        scratch_shapes=[pltpu.VMEM((LANES + SUM_ROWS, t_kv), BF16)],
        compiler_params=_params(("parallel", "parallel", "arbitrary")),
        name="diff_attn",
    )(q, k_all, v_all, row(lq1), row(lk1), row(lq2), row(lk2), row(subln))


def _gate_out_kernel(x_ref, ymix_ref, gq_ref, mk_ref, mv_ref, wout_ref, nw_ref, *out_refs):
    gq = gq_ref[0]
    gate = gq[:, :BRANCH].astype(F32)
    heads = []
    for h in range(MEM_HEADS):
        sl = slice(h * MEM_HD, (h + 1) * MEM_HD)
        qh = gq[:, BRANCH + h * MEM_HD:BRANCH + (h + 1) * MEM_HD]
        s = _mm_nt(qh, mk_ref[0][:, sl]) * (MEM_HD ** -0.5)
        e = jnp.exp(s - jnp.max(s, axis=-1, keepdims=True))
        prob = e * (1.0 / jnp.sum(e, axis=-1, keepdims=True))
        heads.append(_mm(prob, mv_ref[0][:, sl]))
    branch = jnp.concatenate([ymix_ref[0].astype(F32)] + heads, axis=-1)
    act = (branch * (gate * _sigmoid(gate))).astype(BF16)
    xn = x_ref[0] + jnp.dot(act, wout_ref[...], preferred_element_type=F32)
    normed_ref = out_refs[-1]
    normed_ref[0] = (_rms(xn) * nw_ref[...]).astype(normed_ref.dtype)
    if len(out_refs) == 2:
        out_refs[0][0] = xn


def gate_out(x, ymix, gq, mk, mv, w_out, next_norm_w, last):
    bsz, t, d = x.shape
    bm = min(GATE_ROWS, t)
    assert t % bm == 0
    tokb = lambda w: pl.BlockSpec((1, bm, w), lambda b, i: (b, i, 0))
    mem = pl.BlockSpec((1, N_MEM, MEM_W), lambda b, i: (b, 0, 0))
    if last:
        out_specs, out_shape = tokb(d), jax.ShapeDtypeStruct((bsz, t, d), F32)
    else:
        out_specs = [tokb(d), tokb(d)]
        out_shape = [jax.ShapeDtypeStruct((bsz, t, d), F32), jax.ShapeDtypeStruct((bsz, t, d), BF16)]
    return pl.pallas_call(
        _gate_out_kernel,
        grid=(bsz, t // bm),
        in_specs=[tokb(d), tokb(MIX_W), tokb(BRANCH + MEM_W), mem, mem,
                  pl.BlockSpec((BRANCH, d), lambda b, i: (0, 0)),
                  pl.BlockSpec((1, d), lambda b, i: (0, 0))],
        out_specs=out_specs,
        out_shape=out_shape,
        compiler_params=_params(("parallel", "parallel")),
        name="gate_out",
    )(x, ymix, gq, mk, mv, w_out, next_norm_w.reshape(1, d))


def _mem_kv(mem, norm_w, wk, wv):
    bsz = mem.shape[0]
    hm = rmsnorm_bf16(mem.reshape(bsz * N_MEM, D_MODEL), norm_w)
    mk = proj(hm, wk.astype(BF16), MEM_W, F32)
    mv = proj(hm, wv.astype(BF16), MEM_W, F32)
    return mk.reshape(bsz, N_MEM, MEM_W), mv.reshape(bsz, N_MEM, MEM_W)


def _rwkv_layer(x, h, mk, mv, prev_row, s0, w_out, next_norm_w, w_in, mu, w0, w_up, a0, a_up,
                k_k, k_a, r_k, ln_w, ln_b, ts, cl):
    bsz, t, d = x.shape
    w_p = jnp.pad(w_in[:, :RW_PROJ], ((0, 0), (0, RW_PROJ_PAD - RW_PROJ))).astype(BF16)
    w_gq = jnp.concatenate([w_in[:, RW_PROJ + MEM_W:], w_in[:, RW_PROJ:RW_PROJ + MEM_W]], axis=1).astype(BF16)
    p = proj(h, w_p, RW_PROJ_PAD // 2, BF16).reshape(bsz, t, RW_PROJ_PAD)
    gq = proj(h, w_gq, (BRANCH + MEM_W) // 2, BF16).reshape(bsz, t, BRANCH + MEM_W)
    p_last = proj(h.reshape(bsz, t, d)[:, t - 1], w_p, RW_PROJ_PAD // 2, F32)[:, :RW_PROJ]
    zero = jnp.zeros((LORA, MIX_W), F32)
    wup = jnp.concatenate([w_up, zero], axis=0).astype(BF16)
    aup = jnp.concatenate([zero, a_up], axis=0).astype(BF16)
    y_mix, z_t = rwkv_core(p, prev_row.reshape(bsz, 1, RW_PROJ), _pair_state_in(s0), mu, w0, a0, k_k, k_a,
                           r_k.reshape(-1), ln_w, ln_b, wup, aup, ts, cl)
    x_new, h_new = gate_out(x, y_mix, gq, mk, mv, w_out.astype(BF16), next_norm_w, last=False)
    return x_new, h_new.reshape(bsz * t, d), _pair_state_out(z_t), p_last


def _round_up(n, m):
    return (n + m - 1) // m * m


def _diff_layer(x, h, mk, mv, k_past, v_past, layer_idx, w_out, final_w, w_in, lq1, lk1, lq2, lk2,
                subln, tq, tk):
    bsz, t, d = x.shape
    cols = lambda lo, hi: w_in[:, lo:hi].astype(BF16)
    q = proj(h, cols(0, MIX_W), MIX_W, BF16).reshape(bsz, t, MIX_W)
    k4, k = proj_heads(h, cols(MIX_W, 2 * MIX_W))
    v4, v = proj_heads(h, cols(2 * MIX_W, 3 * MIX_W))
    k, v = k.reshape(bsz, t, MIX_W), v.reshape(bsz, t, MIX_W)
    w_gq = jnp.concatenate([w_in[:, 3 * MIX_W + MEM_W:], w_in[:, 3 * MIX_W:3 * MIX_W + MEM_W]], axis=1).astype(BF16)
    gq = proj(h, w_gq, (BRANCH + MEM_W) // 2, BF16).reshape(bsz, t, BRANCH + MEM_W)
    past = 0 if k_past is None else k_past.shape[1]
    t_valid = past + t
    t_q, t_kv = _round_up(t, tq), _round_up(t_valid, tk)
    pad_rows = lambda a, n: a if n == 0 else jnp.pad(a, ((0, 0), (0, n), (0, 0)))
    k_all, v_all = k, v
    if k_past is not None:
        k_all = jnp.concatenate([k_past.reshape(bsz, past, MIX_W).astype(BF16), k], axis=1)
        v_all = jnp.concatenate([v_past.reshape(bsz, past, MIX_W).astype(BF16), v], axis=1)
    lam_init = 0.8 - 0.6 * math.exp(-0.3 * layer_idx)
    o = diff_attn(pad_rows(q, t_q - t), pad_rows(k_all, t_kv - t_valid), pad_rows(v_all, t_kv - t_valid),
                  past, t_valid, lq1, lk1, lq2, lk2, subln, lam_init, tq, tk)[:, :t]
    y = gate_out(x, o, gq, mk, mv, w_out.astype(BF16), final_w, last=True)
    return y, k4.reshape(bsz, t, DF_HEADS, 2 * DF_HD), v4.reshape(bsz, t, DF_HEADS, DF_VD)


def kernel(x_prompt, mem_prompt, x_sample, state_rwkv, state_shift, cache_k, cache_v, cache_mem_k, cache_mem_v, norm_w, mem_norm_w, w_mem_k, w_mem_v, w_out, final_norm_w, rw_in, rw_mu, rw_w0, rw_w_up, rw_a0, rw_a_up, rw_k_k, rw_k_a, rw_r_k, rw_ln_w, rw_ln_b, df_in, df_lq1, df_lk1, df_lq2, df_lk2, df_subln):
    bp, tp, d = x_prompt.shape
    bs, tsm, _ = x_sample.shape
    mem_s = lambda m: m.reshape(bs, N_MEM, MEM_W)
    mk0, mv0 = _mem_kv(mem_prompt, mem_norm_w[0], w_mem_k[0], w_mem_v[0])
    mk1, mv1 = _mem_kv(mem_prompt, mem_norm_w[1], w_mem_k[1], w_mem_v[1])

    rw = (rw_in[0], rw_mu[0], rw_w0[0], rw_w_up[0], rw_a0[0], rw_a_up[0], rw_k_k[0], rw_k_a[0], rw_r_k[0],
          rw_ln_w[0], rw_ln_b[0])
    hp = rmsnorm_bf16(x_prompt.reshape(bp * tp, d), norm_w[0])
    xp, hp, p_s, p_shift = _rwkv_layer(
        x_prompt, hp, mk0, mv0, jnp.zeros((bp, RW_PROJ), F32), jnp.zeros((bp, RW_HEADS, RW_HD, RW_HD), F32),
        w_out[0], norm_w[1], *rw, ts=RW_STEP, cl=RW_CHUNK)
    hs = rmsnorm_bf16(x_sample.reshape(bs * tsm, d), norm_w[0])
    xs, hs, s_s, s_shift = _rwkv_layer(
        x_sample, hs, mem_s(cache_mem_k[0]), mem_s(cache_mem_v[0]), state_shift[0], state_rwkv[0],
        w_out[0], norm_w[1], *rw, ts=tsm, cl=tsm)

    df = (df_in[0], df_lq1[0], df_lk1[0], df_lq2[0], df_lk2[0], df_subln[0])
    y_prompt, p_k, p_v = _diff_layer(xp, hp, mk1, mv1, None, None, 1, w_out[1], final_norm_w, *df,
                                     tq=ATTN_Q_TILE, tk=ATTN_K_TILE)
    t_kv_s = _round_up(cache_k.shape[2] + tsm, LANES)
    y_sample, s_k, s_v = _diff_layer(xs, hs, mem_s(cache_mem_k[1]), mem_s(cache_mem_v[1]), cache_k[0], cache_v[0], 1,
                                     w_out[1], final_norm_w, *df, tq=LANES, tk=t_kv_s)

    mem4 = lambda m: m.reshape(bp, N_MEM, MEM_HEADS, MEM_HD)
    return (y_prompt, y_sample, p_s[None], p_shift[None], p_k[None], p_v[None],
            jnp.stack([mem4(mk0), mem4(mk1)]), jnp.stack([mem4(mv0), mem4(mv1)]),
            s_s[None], s_shift[None], s_k[None], s_v[None])
```

```python
import functools
import math

import jax
import jax.numpy as jnp
from jax import lax
from jax.experimental import pallas as pl
from jax.experimental.pallas import tpu as pltpu

D_MODEL = 1024
CHUNK = 64
CHUNK_SHIFT = 6
N_MEM = 256
BRANCH = 2 * D_MODEL
MIX_W = 3 * D_MODEL // 2
MEM_W = D_MODEL // 2
MEM_HEADS = 4
MEM_HD = MEM_W // MEM_HEADS
RW_HD = 64
RW_HEADS = MIX_W // RW_HD
LORA = 64
RW_PROJ = 3 * MIX_W + 2 * LORA
DF_HD = 64
DF_HEADS = MIX_W // (2 * DF_HD)
DF_VD = 2 * DF_HD
EPS = 1e-6
GN_EPS = 64e-5
NEG_INF = -1e30
LOG2_E = 1.4426950408889634

LANES = 128
RW_PAIRS = RW_HEADS // 2
RW_PROJ_PAD = 4864
V7X_VMEM_LIMIT = 56 * 1024 * 1024

ROW_BLOCK = 1024
GATE_ROWS = 512
RW_CHUNK = 64
RW_STEP = 2 * RW_CHUNK
ATTN_Q_TILE = 1024
ATTN_K_TILE = 512

BF16 = jnp.bfloat16
F32 = jnp.float32


def _mm(a, b):
    return jnp.dot(a.astype(BF16), b.astype(BF16), preferred_element_type=F32)


def _mm_nt(a, b):
    return lax.dot_general(a.astype(BF16), b.astype(BF16), (((1,), (1,)), ((), ())),
                           preferred_element_type=F32)


def _mm_tn(a, b):
    return lax.dot_general(a.astype(BF16), b.astype(BF16), (((0,), (0,)), ((), ())),
                           preferred_element_type=F32)


def _sigmoid(x):
    return 0.5 * jnp.tanh(0.5 * x) + 0.5


def _softplus(x):
    return jnp.maximum(x, 0.0) + jnp.log(1.0 + jnp.exp(-jnp.abs(x)))


def _rms(x):
    return x * lax.rsqrt(jnp.mean(x * x, axis=-1, keepdims=True) + EPS)


def _params(sem):
    return pltpu.CompilerParams(dimension_semantics=sem, vmem_limit_bytes=V7X_VMEM_LIMIT)


def _rmsnorm_kernel(x_ref, nw_ref, o_ref):
    o_ref[...] = (_rms(x_ref[...]) * nw_ref[...]).astype(o_ref.dtype)


def rmsnorm_bf16(x, norm_w):
    n, d = x.shape
    bm = min(ROW_BLOCK, n)
    assert n % bm == 0
    return pl.pallas_call(
        _rmsnorm_kernel,
        grid=(n // bm,),
        in_specs=[pl.BlockSpec((bm, d), lambda i: (i, 0)), pl.BlockSpec((1, d), lambda i: (0, 0))],
        out_specs=pl.BlockSpec((bm, d), lambda i: (i, 0)),
        out_shape=jax.ShapeDtypeStruct((n, d), BF16),
        compiler_params=_params(("parallel",)),
        name="rmsnorm",
    )(x, norm_w.reshape(1, d))


def _proj_kernel(h_ref, w_ref, o_ref):
    o_ref[...] = jnp.dot(h_ref[...], w_ref[...], preferred_element_type=F32).astype(o_ref.dtype)


def proj(h, w, bn, out_dtype):
    n, d = h.shape
    nout = w.shape[1]
    bm = min(ROW_BLOCK, n)
    assert n % bm == 0 and nout % bn == 0
    return pl.pallas_call(
        _proj_kernel,
        grid=(nout // bn, n // bm),
        in_specs=[pl.BlockSpec((bm, d), lambda j, i: (i, 0)),
                  pl.BlockSpec((d, bn), lambda j, i: (0, j))],
        out_specs=pl.BlockSpec((bm, bn), lambda j, i: (i, j)),
        out_shape=jax.ShapeDtypeStruct((n, nout), out_dtype),
        compiler_params=_params(("parallel", "parallel")),
        name="proj",
    )(h, w)


def _proj_heads_kernel(h_ref, w_ref, o4_ref, ob_ref):
    res = jnp.dot(h_ref[...], w_ref[...], preferred_element_type=F32)
    ob_ref[...] = res.astype(BF16)
    o4_ref[...] = res.reshape(res.shape[0], DF_HEADS, LANES)


def proj_heads(h, w):
    n, d = h.shape
    bm = min(ROW_BLOCK, n)
    assert n % bm == 0 and w.shape[1] == MIX_W
    return pl.pallas_call(
        _proj_heads_kernel,
        grid=(n // bm,),
        in_specs=[pl.BlockSpec((bm, d), lambda i: (i, 0)), pl.BlockSpec((d, MIX_W), lambda i: (0, 0))],
        out_specs=[pl.BlockSpec((bm, DF_HEADS, LANES), lambda i: (i, 0, 0)),
                   pl.BlockSpec((bm, MIX_W), lambda i: (i, 0))],
        out_shape=[jax.ShapeDtypeStruct((n, DF_HEADS, LANES), F32), jax.ShapeDtypeStruct((n, MIX_W), BF16)],
        compiler_params=_params(("parallel",)),
        name="proj_heads",
    )(h, w)


def _split3(x):
    hi = x.astype(BF16)
    r1 = x - hi.astype(F32)
    mid = r1.astype(BF16)
    lo = (r1 - mid.astype(F32)).astype(BF16)
    return hi, mid, lo


def _rwkv_kernel(r_ref, k_ref, v_ref, pp_ref, prev_ref, mu_ref, w0_ref, a0_ref, kk_ref, ka_ref, rk_ref,
                 lnw_ref, lnb_ref, wup_ref, aup_ref, z0_ref,
                 y_ref, zout_ref,
                 z_scr, carry, *, ts, cl):
    nc = ts // cl
    rb = 2 * cl
    c_idx = pl.program_id(1)
    k_off, v_off, pp_off = MIX_W, 2 * MIX_W, 3 * MIX_W

    @pl.when(c_idx == 0)
    def _init():
        z_scr[...] = z0_ref[0]
        carry[...] = prev_ref[0]

    def shifted(x_ref, lo):
        x = x_ref[0].astype(F32)
        hi = lo + x.shape[1]
        row = lax.broadcasted_iota(jnp.int32, x.shape, 0)
        prev = jnp.where(row == 0, carry[:, lo:hi], pltpu.roll(x, 1, 0))
        carry[:, lo:hi] = x[ts - 1:ts, :]
        return x + mu_ref[:, lo:hi] * (prev - x)

    r = shifted(r_ref, 0)
    k = shifted(k_ref, k_off)
    v = shifted(v_ref, v_off)
    pp = shifted(pp_ref, pp_off)

    lw = w0_ref[...] + _mm(jnp.tanh(pp), wup_ref[...])
    log_w = -_softplus(-lw) - 0.5
    d = -jnp.exp(log_w)
    a = _sigmoid(a0_ref[...] + _mm(pp, aup_ref[...]))

    li = lax.broadcasted_iota(jnp.int32, (LANES, LANES), 0)
    lj = lax.broadcasted_iota(jnp.int32, (LANES, LANES), 1)
    seg = jnp.where((li >> 6) == (lj >> 6), 1.0, 0.0).astype(BF16)
    pair_lanes = [slice(g * LANES, (g + 1) * LANES) for g in range(RW_PAIRS)]

    def head_sum(x):
        return jnp.concatenate([_mm(x[:, lg], seg) for lg in pair_lanes], axis=1)

    kk = k * kk_ref[...]
    kk = kk * lax.rsqrt(jnp.maximum(head_sum(kk * kk), 1e-24))
    k2 = k * (1.0 + (a - 1.0) * ka_ref[...])
    b = kk * a
    bonus = head_sum(r * k2 * rk_ref[...]) * v

    ci = lax.broadcasted_iota(jnp.int32, (cl, cl), 0)
    cj = lax.broadcasted_iota(jnp.int32, (cl, cl), 1)
    tril = jnp.where(ci >= cj, 1.0, 0.0).astype(BF16)
    rows = [slice(c0 * cl, (c0 + 1) * cl) for c0 in range(nc)]
    parts = _split3(d)
    cs_c = [sum(jnp.dot(tril, part[sl], preferred_element_type=F32) for part in parts) for sl in rows]
    pc_c = [jnp.exp(x[cl - 1:cl, :]) for x in cs_c]
    cat_rows = lambda xs: xs[0] if nc == 1 else jnp.concatenate(xs, axis=0)
    cs = cat_rows(cs_c)
    pc = cat_rows([jnp.broadcast_to(x, (cl, MIX_W)) for x in pc_c])
    e_nc = jnp.exp(-cs)
    at = -kk * jnp.exp(cs - d)
    rt = r * jnp.exp(cs)
    bt = b * e_nc
    kt = k2 * e_nc
    bh = bt * pc
    kh = kt * pc

    lane_c = lax.broadcasted_iota(jnp.int32, (cl, LANES), 1)
    first = lane_c < RW_HD
    si = lax.broadcasted_iota(jnp.int32, (rb, rb), 0)
    sj = lax.broadcasted_iota(jnp.int32, (rb, rb), 1)
    strict = si > sj
    incl = si >= sj
    eye = jnp.where(si == sj, 1.0, 0.0)

    def level_mask(sh):
        return (((si >> (sh + 1)) == (sj >> (sh + 1)))
                & (((si >> sh) & 1) == 1) & (((sj >> sh) & 1) == 0))

    n_levels = int(math.log2(cl))

    chains = [(sl, lg) for sl in rows for lg in pair_lanes]
    each = lambda fn, *lists: [fn(*xs) for xs in zip(*lists)]

    def stacked(x):
        def one(sl, lg):
            t = x[sl, lg]
            return jnp.concatenate([jnp.where(first, t, 0.0), jnp.where(first, 0.0, t)], axis=0)
        return [one(sl, lg) for sl, lg in chains]

    ats, rts, bts, kts, bhs, khs, vs = map(stacked, (at, rt, bt, kt, bh, kh, v))
    if rb == LANES:
        g = each(lambda a_, r_, b_, k_: _mm_nt(jnp.concatenate([a_, r_], axis=0),
                                               jnp.concatenate([b_, k_], axis=0)), ats, rts, bts, kts)
        g_ab, g_ak = [x[:rb, :rb] for x in g], [x[:rb, rb:] for x in g]
        g_rb, g_rk = [x[rb:, :rb] for x in g], [x[rb:, rb:] for x in g]
    else:
        g_ab, g_ak = each(_mm_nt, ats, bts), each(_mm_nt, ats, kts)
        g_rb, g_rk = each(_mm_nt, rts, bts), each(_mm_nt, rts, kts)
    a_ab = each(lambda x: jnp.where(strict, x, 0.0), g_ab)
    a_ak = each(lambda x: jnp.where(strict, x, 0.0), g_ak)
    a_rb = each(lambda x: jnp.where(incl, x, 0.0), g_rb)
    a_rk = each(lambda x: jnp.where(incl, x, 0.0), g_rk)
    t = each(lambda x: eye + jnp.where(level_mask(0), x, 0.0), a_ab)
    for sh in range(1, n_levels):
        lm = level_mask(sh)
        ta = each(lambda t_, x: _mm(t_, jnp.where(lm, x, 0.0)), t, a_ab)
        t = each(lambda t_, x: t_ + _mm(x, t_), t, ta)
    w = each(_mm, a_ak, vs)
    tw = each(lambda t_, a_, w_: _mm(t_, jnp.concatenate([a_, w_], axis=1)), t, ats, w)
    ap, vp = [x[:, :LANES] for x in tw], [x[:, LANES:] for x in tw]
    rp = each(lambda r_, a_, x: r_ + _mm(a_, x), rts, a_rb, ap)
    if rb == LANES:
        yp = each(lambda a_, c_, x, y_: _mm(jnp.concatenate([a_, c_], axis=1),
                                            jnp.concatenate([x, y_], axis=0)), a_rb, a_rk, vp, vs)
    else:
        yp = each(lambda a_, c_, x, y_: _mm(a_, x) + _mm(c_, y_), a_rb, a_rk, vp, vs)
    mn = each(_mm_tn, bhs, tw)
    diag = li == lj
    pcs = [x[:, lg] for x in pc_c for lg in pair_lanes]
    mm = each(lambda x, p: jnp.where(diag, jnp.broadcast_to(p, (LANES, LANES)), 0.0) + x[:, :LANES], mn, pcs)
    nn = each(lambda x, k_, y_: x[:, LANES:] + _mm_tn(k_, y_), mn, khs, vs)

    z = [z_scr[gi] for gi in range(RW_PAIRS)]
    y_rows = []
    for c0 in range(nc):
        base = c0 * RW_PAIRS
        y_st = [_mm(rp[base + gi], z[gi]) + yp[base + gi] for gi in range(RW_PAIRS)]
        z = [_mm(mm[base + gi], z[gi]) + nn[base + gi] for gi in range(RW_PAIRS)]
        y_rows.append(jnp.concatenate([x[:cl] + x[cl:] for x in y_st], axis=1))
    for gi in range(RW_PAIRS):
        z_scr[gi] = z[gi]
    y = cat_rows(y_rows)

    inv_hd = 1.0 / RW_HD
    yc = y - head_sum(y) * inv_hd
    var = head_sum(yc * yc) * inv_hd
    y_ref[0] = (yc * lax.rsqrt(var + GN_EPS) * lnw_ref[...] + lnb_ref[...] + bonus).astype(y_ref.dtype)

    @pl.when(c_idx == pl.num_programs(1) - 1)
    def _fin():
        for gi in range(RW_PAIRS):
            zout_ref[0, gi] = z[gi]


def rwkv_core(p, prev_row, z0, mu, w0, a0, k_k, k_a, r_k, ln_w, ln_b, wup, aup, ts, cl):
    bsz, t, _ = p.shape
    assert t % ts == 0 and ts % cl == 0
    tok = lambda off: pl.BlockSpec((1, ts, MIX_W), lambda b, c, off=off: (b, c, off))
    chan = pl.BlockSpec((1, MIX_W), lambda b, c: (0, 0))
    lora = pl.BlockSpec((LANES, MIX_W), lambda b, c: (0, 0))
    state = pl.BlockSpec((1, RW_PAIRS, LANES, LANES), lambda b, c: (b, 0, 0, 0))
    in_specs = [tok(0), tok(1), tok(2),
                pl.BlockSpec((1, ts, LANES), lambda b, c: (b, c, 3 * MIX_W // LANES)),
                pl.BlockSpec((1, 1, RW_PROJ), lambda b, c: (b, 0, 0)),
                pl.BlockSpec((1, RW_PROJ), lambda b, c: (0, 0)),
                chan, chan, chan, chan, chan, chan, chan, lora, lora, state]
    row = lambda a: a.reshape(1, -1)
    return pl.pallas_call(
        functools.partial(_rwkv_kernel, ts=ts, cl=cl),
        grid=(bsz, t // ts),
        in_specs=in_specs,
        out_specs=[pl.BlockSpec((1, ts, MIX_W), lambda b, c: (b, c, 0)), state],
        out_shape=[jax.ShapeDtypeStruct((bsz, t, MIX_W), BF16),
                   jax.ShapeDtypeStruct((bsz, RW_PAIRS, LANES, LANES), F32)],
        scratch_shapes=[pltpu.VMEM((RW_PAIRS, LANES, LANES), F32), pltpu.VMEM((1, RW_PROJ), F32)],
        compiler_params=_params(("parallel", "arbitrary")),
        name="rwkv_core",
    )(p, p, p, p, prev_row, row(mu), row(w0), row(a0), row(k_k), row(k_a), row(r_k), row(ln_w), row(ln_b),
      wup, aup, z0)


def _pair_state_in(s):
    bsz = s.shape[0]
    st = jnp.swapaxes(s.astype(F32), -1, -2).reshape(bsz, RW_PAIRS, 2, RW_HD, RW_HD)
    z = st[:, :, :, :, None, :] * jnp.eye(2, dtype=F32)[None, None, :, None, :, None]
    return z.reshape(bsz, RW_PAIRS, LANES, LANES)


def _pair_state_out(z):
    bsz = z.shape[0]
    z6 = z.reshape(bsz, RW_PAIRS, 2, RW_HD, 2, RW_HD)
    st = jnp.stack([z6[:, :, 0, :, 0, :], z6[:, :, 1, :, 1, :]], axis=2)
    return jnp.swapaxes(st.reshape(bsz, RW_HEADS, RW_HD, RW_HD), -1, -2)


SUM_ROWS = 16


def _attn_stream_width(tq):
    return min(tq, 2 * LANES)


def _diff_attn_kernel(q_ref, k_ref, v_ref, lq1, lk1, lq2, lk2, sub_ref, o_ref, vbt,
                      *, tq, tk, t_kv, t_valid, past, lam_init):
    qi = pl.program_id(2)
    kb = k_ref.at[0]

    @pl.when(qi == 0)
    def _stage():
        ei = lax.broadcasted_iota(jnp.int32, (LANES, LANES), 0)
        ej = lax.broadcasted_iota(jnp.int32, (LANES, LANES), 1)
        eye = jnp.where(ei == ej, 1.0, 0.0).astype(BF16)

        def body(c, _):
            start = pl.multiple_of(c * tk, tk)
            vt = lax.dot_general(eye, v_ref[0, pl.ds(start, tk), :], (((1,), (1,)), ((), ())),
                                 preferred_element_type=F32)
            vbt[:LANES, pl.ds(start, tk)] = vt.astype(BF16)
            return 0

        lax.fori_loop(0, t_kv // tk, body, 0)
        ones_row = lax.broadcasted_iota(jnp.int32, (SUM_ROWS, t_kv), 0) == 0
        vbt[LANES:, :] = jnp.where(ones_row, 1.0, 0.0).astype(BF16)

    lam = (jnp.exp(jnp.sum(lq1[...] * lk1[...], axis=-1, keepdims=True))
           - jnp.exp(jnp.sum(lq2[...] * lk2[...], axis=-1, keepdims=True)) + lam_init)

    lane = lax.broadcasted_iota(jnp.int32, (tq, LANES), 1)
    q = q_ref[0].astype(F32) * (DF_HD ** -0.5 * LOG2_E)
    q1 = jnp.where(lane < DF_HD, q, 0.0).astype(BF16)
    q2 = jnp.where(lane >= DF_HD, q, 0.0).astype(BF16)

    q0 = past + qi * tq
    lim_first = jnp.minimum(((q0 >> CHUNK_SHIFT) + 1) << CHUNK_SHIFT, t_valid)
    lim_last = jnp.minimum((((q0 + tq - 1) >> CHUNK_SHIFT) + 1) << CHUNK_SHIFT, t_valid)
    n_full = lim_first // tk
    n_tot = (lim_last + tk - 1) // tk

    tw = _attn_stream_width(tq)
    n_blk = tq // tw
    n_str = 2 * n_blk
    q_str = [qh[i * tw:(i + 1) * tw] for qh in (q1, q2) for i in range(n_blk)]
    q_pos = q0 + lax.broadcasted_iota(jnp.int32, (1, tq), 1)
    q_lim = jnp.minimum(((q_pos >> CHUNK_SHIFT) + 1) << CHUNK_SHIFT, t_valid)
    lim_str = [q_lim[:, i * tw:(i + 1) * tw] for _ in range(2) for i in range(n_blk)]
    rows_full = [tk] * n_str

    def tile(j, rows):
        start = pl.multiple_of(j * tk, tk)
        uniq = {n: (kb[pl.ds(start, n), :], vbt[:, pl.ds(start, n)]) for n in sorted(set(rows)) if n}
        uniq[0] = (None, None)
        return start, [uniq[n][0] for n in rows], [uniq[n][1] for n in rows]

    def qk(kcs):
        return [None if kc is None else
                lax.dot_general(kc, qs, (((1,), (1,)), ((), ())), preferred_element_type=F32)
                for kc, qs in zip(kcs, q_str)]

    def softmax_update(s, m_old, start, masked):
        live = [x is not None for x in s]
        if masked:
            k_pos = {n: start + lax.broadcasted_iota(jnp.int32, (n, tw), 0)
                     for n in set(x.shape[0] for x in s if x is not None)}
            s = [jnp.where(k_pos[x.shape[0]] < lim, x, NEG_INF) if ok else None
                 for ok, x, lim in zip(live, s, lim_str)]
        m_new = [jnp.maximum(m, jnp.max(x, axis=0, keepdims=True)) if ok else m
                 for ok, m, x in zip(live, m_old, s)]
        corr = [jnp.exp2(m - mn) if ok else None for ok, m, mn in zip(live, m_old, m_new)]
        p = [jnp.exp2(x - mn) if ok else None for ok, x, mn in zip(live, s, m_new)]
        return m_new, corr, p

    def pv(vcts, p, acc_old, corr):
        return [a if x is None else a * c + jnp.dot(vct, x.astype(BF16), preferred_element_type=F32)
                for vct, a, c, x in zip(vcts, acc_old, corr, p)]

    def single(j, carry, masked, rows=rows_full):
        m_old, acc_old = carry
        start, kcs, vcts = tile(j, rows)
        m_new, corr, p = softmax_update(qk(kcs), m_old, start, masked)
        return tuple(m_new), tuple(pv(vcts, p, acc_old, corr))

    def pair(j2, carry):
        m_old, acc_old = carry
        start_a, kc_a, vct_a = tile(2 * j2, rows_full)
        start_b, kc_b, vct_b = tile(2 * j2 + 1, rows_full)
        s_a = qk(kc_a)
        s_b = qk(kc_b)
        m_a, corr_a, p_a = softmax_update(s_a, m_old, start_a, False)
        acc_a = pv(vct_a, p_a, acc_old, corr_a)
        m_b, corr_b, p_b = softmax_update(s_b, m_a, start_b, False)
        return tuple(m_b), tuple(pv(vct_b, p_b, acc_a, corr_b))

    carry = (tuple(jnp.full((1, tw), NEG_INF, F32) for _ in range(n_str)),
             tuple(jnp.zeros((LANES + SUM_ROWS, tw), F32) for _ in range(n_str)))
    n_pair = n_full // 2
    carry = lax.fori_loop(0, n_pair, pair, carry)
    carry = lax.fori_loop(2 * n_pair, n_full, functools.partial(single, masked=False), carry)
    if past == 0 and tq % tk == 0 and t_valid == t_kv:
        for dt in range(tq // tk):
            rows = [min(max((i + 1) * tw - dt * tk, 0), tk) for _ in range(2) for i in range(n_blk)]
            carry = single(n_full + dt, carry, True, rows)
    else:
        carry = lax.fori_loop(n_full, n_tot, functools.partial(single, masked=True), carry)
    _, acc_fin = carry
    cat = lambda xs: xs[0] if len(xs) == 1 else jnp.concatenate(xs, axis=1)
    l1 = cat([x[LANES:LANES + 1] for x in acc_fin[:n_blk]])
    l2 = cat([x[LANES:LANES + 1] for x in acc_fin[n_blk:]])
    acc1 = cat([x[:LANES] for x in acc_fin[:n_blk]])
    acc2 = cat([x[:LANES] for x in acc_fin[n_blk:]])
    o_t = acc1 * (1.0 / l1) - acc2 * (lam / l2)
    ms = jnp.mean(o_t * o_t, axis=0, keepdims=True)
    o_t = o_t * lax.rsqrt(ms + EPS)
    o_ref[0] = (o_t.T * sub_ref[...] * (1.0 - lam_init)).astype(o_ref.dtype)


def diff_attn(q, k_all, v_all, past, t_valid, lq1, lk1, lq2, lk2, subln, lam_init, tq, tk):
    bsz, t_q, _ = q.shape
    t_kv = k_all.shape[1]
    assert t_q % tq == 0 and t_kv % tk == 0 and t_kv % LANES == 0 and tq % LANES == 0
    kern = functools.partial(_diff_attn_kernel, tq=tq, tk=tk, t_kv=t_kv, t_valid=t_valid,
                             past=past, lam_init=lam_init)
    vec = pl.BlockSpec((1, DF_HD), lambda b, h, i: (0, 0))
    kv = pl.BlockSpec((1, t_kv, LANES), lambda b, h, i: (b, 0, h))
    row = lambda a: a.reshape(1, -1)
    return pl.pallas_call(
        kern,
        grid=(bsz, DF_HEADS, t_q // tq),
        in_specs=[pl.BlockSpec((1, tq, LANES), lambda b, h, i: (b, i, h)), kv, kv,
                  vec, vec, vec, vec, pl.BlockSpec((1, DF_VD), lambda b, h, i: (0, 0))],
        out_specs=pl.BlockSpec((1, tq, LANES), lambda b, h, i: (b, i, h)),
        out_shape=jax.ShapeDtypeStruct((bsz, t_q, MIX_W), BF16),
        scratch_shapes=[pltpu.VMEM((LANES + SUM_ROWS, t_kv), BF16)],
        compiler_params=_params(("parallel", "parallel", "arbitrary")),
        name="diff_attn",
    )(q, k_all, v_all, row(lq1), row(lk1), row(lq2), row(lk2), row(subln))


def _gate_out_kernel(x_ref, ymix_ref, gq_ref, mk_ref, mv_ref, wout_ref, nw_ref, *out_refs):
    gq = gq_ref[0]
    gate = gq[:, :BRANCH].astype(F32)
    heads = []
    for h in range(MEM_HEADS):
        sl = slice(h * MEM_HD, (h + 1) * MEM_HD)
        qh = gq[:, BRANCH + h * MEM_HD:BRANCH + (h + 1) * MEM_HD]
        s = _mm_nt(qh, mk_ref[0][:, sl]) * (MEM_HD ** -0.5)
        e = jnp.exp(s - jnp.max(s, axis=-1, keepdims=True))
        prob = e * (1.0 / jnp.sum(e, axis=-1, keepdims=True))
        heads.append(_mm(prob, mv_ref[0][:, sl]))
    branch = jnp.concatenate([ymix_ref[0].astype(F32)] + heads, axis=-1)
    act = (branch * (gate * _sigmoid(gate))).astype(BF16)
    xn = x_ref[0] + jnp.dot(act, wout_ref[...], preferred_element_type=F32)
    normed_ref = out_refs[-1]
    normed_ref[0] = (_rms(xn) * nw_ref[...]).astype(normed_ref.dtype)
    if len(out_refs) == 2:
        out_refs[0][0] = xn


def gate_out(x, ymix, gq, mk, mv, w_out, next_norm_w, last):
    bsz, t, d = x.shape
    bm = min(GATE_ROWS, t)
    assert t % bm == 0
    tokb = lambda w: pl.BlockSpec((1, bm, w), lambda b, i: (b, i, 0))
    mem = pl.BlockSpec((1, N_MEM, MEM_W), lambda b, i: (b, 0, 0))
    if last:
        out_specs, out_shape = tokb(d), jax.ShapeDtypeStruct((bsz, t, d), F32)
    else:
        out_specs = [tokb(d), tokb(d)]
        out_shape = [jax.ShapeDtypeStruct((bsz, t, d), F32), jax.ShapeDtypeStruct((bsz, t, d), BF16)]
    return pl.pallas_call(
        _gate_out_kernel,
        grid=(bsz, t // bm),
        in_specs=[tokb(d), tokb(MIX_W), tokb(BRANCH + MEM_W), mem, mem,
                  pl.BlockSpec((BRANCH, d), lambda b, i: (0, 0)),
                  pl.BlockSpec((1, d), lambda b, i: (0, 0))],
        out_specs=out_specs,
        out_shape=out_shape,
        compiler_params=_params(("parallel", "parallel")),
        name="gate_out",
    )(x, ymix, gq, mk, mv, w_out, next_norm_w.reshape(1, d))


def _mem_kv(mem, norm_w, wk, wv):
    bsz = mem.shape[0]
    hm = rmsnorm_bf16(mem.reshape(bsz * N_MEM, D_MODEL), norm_w)
    mk = proj(hm, wk.astype(BF16), MEM_W, F32)
    mv = proj(hm, wv.astype(BF16), MEM_W, F32)
    return mk.reshape(bsz, N_MEM, MEM_W), mv.reshape(bsz, N_MEM, MEM_W)


def _rwkv_layer(x, h, mk, mv, prev_row, s0, w_out, next_norm_w, w_in, mu, w0, w_up, a0, a_up,
                k_k, k_a, r_k, ln_w, ln_b, ts, cl):
    bsz, t, d = x.shape
    w_p = jnp.pad(w_in[:, :RW_PROJ], ((0, 0), (0, RW_PROJ_PAD - RW_PROJ))).astype(BF16)
    w_gq = jnp.concatenate([w_in[:, RW_PROJ + MEM_W:], w_in[:, RW_PROJ:RW_PROJ + MEM_W]], axis=1).astype(BF16)
    p = proj(h, w_p, RW_PROJ_PAD // 2, BF16).reshape(bsz, t, RW_PROJ_PAD)
    gq = proj(h, w_gq, (BRANCH + MEM_W) // 2, BF16).reshape(bsz, t, BRANCH + MEM_W)
    p_last = proj(h.reshape(bsz, t, d)[:, t - 1], w_p, RW_PROJ_PAD // 2, F32)[:, :RW_PROJ]
    zero = jnp.zeros((LORA, MIX_W), F32)
    wup = jnp.concatenate([w_up, zero], axis=0).astype(BF16)
    aup = jnp.concatenate([zero, a_up], axis=0).astype(BF16)
    y_mix, z_t = rwkv_core(p, prev_row.reshape(bsz, 1, RW_PROJ), _pair_state_in(s0), mu, w0, a0, k_k, k_a,
                           r_k.reshape(-1), ln_w, ln_b, wup, aup, ts, cl)
    x_new, h_new = gate_out(x, y_mix, gq, mk, mv, w_out.astype(BF16), next_norm_w, last=False)
    return x_new, h_new.reshape(bsz * t, d), _pair_state_out(z_t), p_last


def _round_up(n, m):
    return (n + m - 1) // m * m


def _diff_layer(x, h, mk, mv, k_past, v_past, layer_idx, w_out, final_w, w_in, lq1, lk1, lq2, lk2,
                subln, tq, tk):
    bsz, t, d = x.shape
    cols = lambda lo, hi: w_in[:, lo:hi].astype(BF16)
    q = proj(h, cols(0, MIX_W), MIX_W, BF16).reshape(bsz, t, MIX_W)
    k4, k = proj_heads(h, cols(MIX_W, 2 * MIX_W))
    v4, v = proj_heads(h, cols(2 * MIX_W, 3 * MIX_W))
    k, v = k.reshape(bsz, t, MIX_W), v.reshape(bsz, t, MIX_W)
    w_gq = jnp.concatenate([w_in[:, 3 * MIX_W + MEM_W:], w_in[:, 3 * MIX_W:3 * MIX_W + MEM_W]], axis=1).astype(BF16)
    gq = proj(h, w_gq, (BRANCH + MEM_W) // 2, BF16).reshape(bsz, t, BRANCH + MEM_W)
    past = 0 if k_past is None else k_past.shape[1]
    t_valid = past + t
    t_q, t_kv = _round_up(t, tq), _round_up(t_valid, tk)
    pad_rows = lambda a, n: a if n == 0 else jnp.pad(a, ((0, 0), (0, n), (0, 0)))
    k_all, v_all = k, v
    if k_past is not None:
        k_all = jnp.concatenate([k_past.reshape(bsz, past, MIX_W).astype(BF16), k], axis=1)
        v_all = jnp.concatenate([v_past.reshape(bsz, past, MIX_W).astype(BF16), v], axis=1)
    lam_init = 0.8 - 0.6 * math.exp(-0.3 * layer_idx)
    o = diff_attn(pad_rows(q, t_q - t), pad_rows(k_all, t_kv - t_valid), pad_rows(v_all, t_kv - t_valid),
                  past, t_valid, lq1, lk1, lq2, lk2, subln, lam_init, tq, tk)[:, :t]
    y = gate_out(x, o, gq, mk, mv, w_out.astype(BF16), final_w, last=True)
    return y, k4.reshape(bsz, t, DF_HEADS, 2 * DF_HD), v4.reshape(bsz, t, DF_HEADS, DF_VD)


def kernel(x_prompt, mem_prompt, x_sample, state_rwkv, state_shift, cache_k, cache_v, cache_mem_k, cache_mem_v, norm_w, mem_norm_w, w_mem_k, w_mem_v, w_out, final_norm_w, rw_in, rw_mu, rw_w0, rw_w_up, rw_a0, rw_a_up, rw_k_k, rw_k_a, rw_r_k, rw_ln_w, rw_ln_b, df_in, df_lq1, df_lk1, df_lq2, df_lk2, df_subln):
    bp, tp, d = x_prompt.shape
    bs, tsm, _ = x_sample.shape
    mem_s = lambda m: m.reshape(bs, N_MEM, MEM_W)
    mk0, mv0 = _mem_kv(mem_prompt, mem_norm_w[0], w_mem_k[0], w_mem_v[0])
    mk1, mv1 = _mem_kv(mem_prompt, mem_norm_w[1], w_mem_k[1], w_mem_v[1])

    rw = (rw_in[0], rw_mu[0], rw_w0[0], rw_w_up[0], rw_a0[0], rw_a_up[0], rw_k_k[0], rw_k_a[0], rw_r_k[0],
          rw_ln_w[0], rw_ln_b[0])
    hp = rmsnorm_bf16(x_prompt.reshape(bp * tp, d), norm_w[0])
    xp, hp, p_s, p_shift = _rwkv_layer(
        x_prompt, hp, mk0, mv0, jnp.zeros((bp, RW_PROJ), F32), jnp.zeros((bp, RW_HEADS, RW_HD, RW_HD), F32),
        w_out[0], norm_w[1], *rw, ts=RW_STEP, cl=RW_CHUNK)
    hs = rmsnorm_bf16(x_sample.reshape(bs * tsm, d), norm_w[0])
    xs, hs, s_s, s_shift = _rwkv_layer(
        x_sample, hs, mem_s(cache_mem_k[0]), mem_s(cache_mem_v[0]), state_shift[0], state_rwkv[0],
        w_out[0], norm_w[1], *rw, ts=tsm, cl=tsm)

    df = (df_in[0], df_lq1[0], df_lk1[0], df_lq2[0], df_lk2[0], df_subln[0])
    y_prompt, p_k, p_v = _diff_layer(xp, hp, mk1, mv1, None, None, 1, w_out[1], final_norm_w, *df,
                                     tq=ATTN_Q_TILE, tk=ATTN_K_TILE)
    t_kv_s = _round_up(cache_k.shape[2] + tsm, LANES)
    y_sample, s_k, s_v = _diff_layer(xs, hs, mem_s(cache_mem_k[1]), mem_s(cache_mem_v[1]), cache_k[0], cache_v[0], 1,
                                     w_out[1], final_norm_w, *df, tq=LANES, tk=t_kv_s)

    mem4 = lambda m: m.reshape(bp, N_MEM, MEM_HEADS, MEM_HD)
    return (y_prompt, y_sample, p_s[None], p_shift[None], p_k[None], p_v[None],
            jnp.stack([mem4(mk0), mem4(mk1)]), jnp.stack([mem4(mv0), mem4(mv1)]),
            s_s[None], s_shift[None], s_k[None], s_v[None])
```

```python
import functools
import math

import jax
import jax.numpy as jnp
from jax import lax
from jax.experimental import pallas as pl
from jax.experimental.pallas import tpu as pltpu

D_MODEL = 1024
CHUNK = 64
CHUNK_SHIFT = 6
N_MEM = 256
BRANCH = 2 * D_MODEL
MIX_W = 3 * D_MODEL // 2
MEM_W = D_MODEL // 2
MEM_HEADS = 4
MEM_HD = MEM_W // MEM_HEADS
RW_HD = 64
RW_HEADS = MIX_W // RW_HD
LORA = 64
RW_PROJ = 3 * MIX_W + 2 * LORA
DF_HD = 64
DF_HEADS = MIX_W // (2 * DF_HD)
DF_VD = 2 * DF_HD
EPS = 1e-6
GN_EPS = 64e-5
NEG_INF = -1e30
LOG2_E = 1.4426950408889634

LANES = 128
RW_PAIRS = RW_HEADS // 2
RW_PROJ_PAD = 4864
V7X_VMEM_LIMIT = 56 * 1024 * 1024

ROW_BLOCK = 1024
GATE_ROWS = 512
RW_CHUNK = 64
RW_STEP = 2 * RW_CHUNK
ATTN_Q_TILE = 1024
ATTN_K_TILE = 512

BF16 = jnp.bfloat16
F32 = jnp.float32


def _mm(a, b):
    return jnp.dot(a.astype(BF16), b.astype(BF16), preferred_element_type=F32)


def _mm_nt(a, b):
    return lax.dot_general(a.astype(BF16), b.astype(BF16), (((1,), (1,)), ((), ())),
                           preferred_element_type=F32)


def _mm_tn(a, b):
    return lax.dot_general(a.astype(BF16), b.astype(BF16), (((0,), (0,)), ((), ())),
                           preferred_element_type=F32)


def _sigmoid(x):
    return 0.5 * jnp.tanh(0.5 * x) + 0.5


def _softplus(x):
    return jnp.maximum(x, 0.0) + jnp.log(1.0 + jnp.exp(-jnp.abs(x)))


def _rms(x):
    return x * lax.rsqrt(jnp.mean(x * x, axis=-1, keepdims=True) + EPS)


def _params(sem):
    return pltpu.CompilerParams(dimension_semantics=sem, vmem_limit_bytes=V7X_VMEM_LIMIT)


def _rmsnorm_kernel(x_ref, nw_ref, o_ref):
    o_ref[...] = (_rms(x_ref[...]) * nw_ref[...]).astype(o_ref.dtype)


def rmsnorm_bf16(x, norm_w):
    n, d = x.shape
    bm = min(ROW_BLOCK, n)
    assert n % bm == 0
    return pl.pallas_call(
        _rmsnorm_kernel,
        grid=(n // bm,),
        in_specs=[pl.BlockSpec((bm, d), lambda i: (i, 0)), pl.BlockSpec((1, d), lambda i: (0, 0))],
        out_specs=pl.BlockSpec((bm, d), lambda i: (i, 0)),
        out_shape=jax.ShapeDtypeStruct((n, d), BF16),
        compiler_params=_params(("parallel",)),
        name="rmsnorm",
    )(x, norm_w.reshape(1, d))


def _proj_kernel(h_ref, w_ref, o_ref):
    o_ref[...] = jnp.dot(h_ref[...], w_ref[...], preferred_element_type=F32).astype(o_ref.dtype)


def proj(h, w, bn, out_dtype):
    n, d = h.shape
    nout = w.shape[1]
    bm = min(ROW_BLOCK, n)
    assert n % bm == 0 and nout % bn == 0
    return pl.pallas_call(
        _proj_kernel,
        grid=(nout // bn, n // bm),
        in_specs=[pl.BlockSpec((bm, d), lambda j, i: (i, 0)),
                  pl.BlockSpec((d, bn), lambda j, i: (0, j))],
        out_specs=pl.BlockSpec((bm, bn), lambda j, i: (i, j)),
        out_shape=jax.ShapeDtypeStruct((n, nout), out_dtype),
        compiler_params=_params(("parallel", "parallel")),
        name="proj",
    )(h, w)


def _proj_heads_kernel(h_ref, w_ref, o4_ref, ob_ref):
    res = jnp.dot(h_ref[...], w_ref[...], preferred_element_type=F32)
    ob_ref[...] = res.astype(BF16)
    o4_ref[...] = res.reshape(res.shape[0], DF_HEADS, LANES)


def proj_heads(h, w):
    n, d = h.shape
    bm = min(ROW_BLOCK, n)
    assert n % bm == 0 and w.shape[1] == MIX_W
    return pl.pallas_call(
        _proj_heads_kernel,
        grid=(n // bm,),
        in_specs=[pl.BlockSpec((bm, d), lambda i: (i, 0)),
                  pl.BlockSpec((d, MIX_W), lambda i: (0, 0), pipeline_mode=pl.Buffered(1))],
        out_specs=[pl.BlockSpec((bm, DF_HEADS, LANES), lambda i: (i, 0, 0)),
                   pl.BlockSpec((bm, MIX_W), lambda i: (i, 0))],
        out_shape=[jax.ShapeDtypeStruct((n, DF_HEADS, LANES), F32), jax.ShapeDtypeStruct((n, MIX_W), BF16)],
        compiler_params=_params(("parallel",)),
        name="proj_heads",
    )(h, w)


def _split3(x):
    hi = x.astype(BF16)
    r1 = x - hi.astype(F32)
    mid = r1.astype(BF16)
    lo = (r1 - mid.astype(F32)).astype(BF16)
    return hi, mid, lo


def _rwkv_kernel(r_ref, k_ref, v_ref, pp_ref, prev_ref, mu_ref, w0_ref, a0_ref, kk_ref, ka_ref, rk_ref,
                 lnw_ref, lnb_ref, wup_ref, aup_ref, z0_ref,
                 y_ref, zout_ref,
                 z_scr, carry, *, ts, cl):
    nc = ts // cl
    rb = 2 * cl
    c_idx = pl.program_id(1)
    k_off, v_off, pp_off = MIX_W, 2 * MIX_W, 3 * MIX_W

    @pl.when(c_idx == 0)
    def _init():
        z_scr[...] = z0_ref[0]
        carry[...] = prev_ref[0]

    def shifted(x_ref, lo):
        x = x_ref[0].astype(F32)
        hi = lo + x.shape[1]
        row = lax.broadcasted_iota(jnp.int32, x.shape, 0)
        prev = jnp.where(row == 0, carry[:, lo:hi], pltpu.roll(x, 1, 0))
        carry[:, lo:hi] = x[ts - 1:ts, :]
        return x + mu_ref[:, lo:hi] * (prev - x)

    r = shifted(r_ref, 0)
    k = shifted(k_ref, k_off)
    v = shifted(v_ref, v_off)
    pp = shifted(pp_ref, pp_off)

    lw = w0_ref[...] + _mm(jnp.tanh(pp), wup_ref[...])
    log_w = -_softplus(-lw) - 0.5
    d = -jnp.exp(log_w)
    a = _sigmoid(a0_ref[...] + _mm(pp, aup_ref[...]))

    li = lax.broadcasted_iota(jnp.int32, (LANES, LANES), 0)
    lj = lax.broadcasted_iota(jnp.int32, (LANES, LANES), 1)
    seg = jnp.where((li >> 6) == (lj >> 6), 1.0, 0.0).astype(BF16)
    pair_lanes = [slice(g * LANES, (g + 1) * LANES) for g in range(RW_PAIRS)]

    def head_sum(x):
        return jnp.concatenate([_mm(x[:, lg], seg) for lg in pair_lanes], axis=1)

    kk = k * kk_ref[...]
    kk = kk * lax.rsqrt(jnp.maximum(head_sum(kk * kk), 1e-24))
    k2 = k * (1.0 + (a - 1.0) * ka_ref[...])
    b = kk * a
    bonus = head_sum(r * k2 * rk_ref[...]) * v

    ci = lax.broadcasted_iota(jnp.int32, (cl, cl), 0)
    cj = lax.broadcasted_iota(jnp.int32, (cl, cl), 1)
    tril = jnp.where(ci >= cj, 1.0, 0.0).astype(BF16)
    rows = [slice(c0 * cl, (c0 + 1) * cl) for c0 in range(nc)]
    parts = _split3(d)
    cs_c = [sum(jnp.dot(tril, part[sl], preferred_element_type=F32) for part in parts) for sl in rows]
    pc_c = [jnp.exp(x[cl - 1:cl, :]) for x in cs_c]
    cat_rows = lambda xs: xs[0] if nc == 1 else jnp.concatenate(xs, axis=0)
    cs = cat_rows(cs_c)
    pc = cat_rows([jnp.broadcast_to(x, (cl, MIX_W)) for x in pc_c])
    e_nc = jnp.exp(-cs)
    at = -kk * jnp.exp(cs - d)
    rt = r * jnp.exp(cs)
    bt = b * e_nc
    kt = k2 * e_nc
    bh = bt * pc
    kh = kt * pc

    lane_c = lax.broadcasted_iota(jnp.int32, (cl, LANES), 1)
    first = lane_c < RW_HD
    si = lax.broadcasted_iota(jnp.int32, (rb, rb), 0)
    sj = lax.broadcasted_iota(jnp.int32, (rb, rb), 1)
    strict = si > sj
    incl = si >= sj
    eye = jnp.where(si == sj, 1.0, 0.0)

    def level_mask(sh):
        return (((si >> (sh + 1)) == (sj >> (sh + 1)))
                & (((si >> sh) & 1) == 1) & (((sj >> sh) & 1) == 0))

    n_levels = int(math.log2(cl))

    chains = [(sl, lg) for sl in rows for lg in pair_lanes]
    each = lambda fn, *lists: [fn(*xs) for xs in zip(*lists)]

    def stacked(x):
        def one(sl, lg):
            t = x[sl, lg]
            return jnp.concatenate([jnp.where(first, t, 0.0), jnp.where(first, 0.0, t)], axis=0)
        return [one(sl, lg) for sl, lg in chains]

    ats, rts, bts, kts, bhs, khs, vs = map(stacked, (at, rt, bt, kt, bh, kh, v))
    if rb == LANES:
        g = each(lambda a_, r_, b_, k_: _mm_nt(jnp.concatenate([a_, r_], axis=0),
                                               jnp.concatenate([b_, k_], axis=0)), ats, rts, bts, kts)
        g_ab, g_ak = [x[:rb, :rb] for x in g], [x[:rb, rb:] for x in g]
        g_rb, g_rk = [x[rb:, :rb] for x in g], [x[rb:, rb:] for x in g]
    else:
        g_ab, g_ak = each(_mm_nt, ats, bts), each(_mm_nt, ats, kts)
        g_rb, g_rk = each(_mm_nt, rts, bts), each(_mm_nt, rts, kts)
    a_ab = each(lambda x: jnp.where(strict, x, 0.0), g_ab)
    a_ak = each(lambda x: jnp.where(strict, x, 0.0), g_ak)
    a_rb = each(lambda x: jnp.where(incl, x, 0.0), g_rb)
    a_rk = each(lambda x: jnp.where(incl, x, 0.0), g_rk)
    t = each(lambda x: eye + jnp.where(level_mask(0), x, 0.0), a_ab)
    for sh in range(1, n_levels):
        lm = level_mask(sh)
        ta = each(lambda t_, x: _mm(t_, jnp.where(lm, x, 0.0)), t, a_ab)
        t = each(lambda t_, x: t_ + _mm(x, t_), t, ta)
    w = each(_mm, a_ak, vs)
    tw = each(lambda t_, a_, w_: _mm(t_, jnp.concatenate([a_, w_], axis=1)), t, ats, w)
    ap, vp = [x[:, :LANES] for x in tw], [x[:, LANES:] for x in tw]
    rp = each(lambda r_, a_, x: r_ + _mm(a_, x), rts, a_rb, ap)
    if rb == LANES:
        yp = each(lambda a_, c_, x, y_: _mm(jnp.concatenate([a_, c_], axis=1),
                                            jnp.concatenate([x, y_], axis=0)), a_rb, a_rk, vp, vs)
    else:
        yp = each(lambda a_, c_, x, y_: _mm(a_, x) + _mm(c_, y_), a_rb, a_rk, vp, vs)
    mn = each(_mm_tn, bhs, tw)
    diag = li == lj
    pcs = [x[:, lg] for x in pc_c for lg in pair_lanes]
    mm = each(lambda x, p: jnp.where(diag, jnp.broadcast_to(p, (LANES, LANES)), 0.0) + x[:, :LANES], mn, pcs)
    nn = each(lambda x, k_, y_: x[:, LANES:] + _mm_tn(k_, y_), mn, khs, vs)

    z = [z_scr[gi] for gi in range(RW_PAIRS)]
    y_rows = []
    for c0 in range(nc):
        base = c0 * RW_PAIRS
        y_st = [_mm(rp[base + gi], z[gi]) + yp[base + gi] for gi in range(RW_PAIRS)]
        z = [_mm(mm[base + gi], z[gi]) + nn[base + gi] for gi in range(RW_PAIRS)]
        y_rows.append(jnp.concatenate([x[:cl] + x[cl:] for x in y_st], axis=1))
    for gi in range(RW_PAIRS):
        z_scr[gi] = z[gi]
    y = cat_rows(y_rows)

    inv_hd = 1.0 / RW_HD
    yc = y - head_sum(y) * inv_hd
    var = head_sum(yc * yc) * inv_hd
    y_ref[0] = (yc * lax.rsqrt(var + GN_EPS) * lnw_ref[...] + lnb_ref[...] + bonus).astype(y_ref.dtype)

    @pl.when(c_idx == pl.num_programs(1) - 1)
    def _fin():
        for gi in range(RW_PAIRS):
            zout_ref[0, gi] = z[gi]


def rwkv_core(p, prev_row, z0, mu, w0, a0, k_k, k_a, r_k, ln_w, ln_b, wup, aup, ts, cl):
    bsz, t, _ = p.shape
    assert t % ts == 0 and ts % cl == 0
    tok = lambda off: pl.BlockSpec((1, ts, MIX_W), lambda b, c, off=off: (b, c, off))
    chan = pl.BlockSpec((1, MIX_W), lambda b, c: (0, 0))
    lora = pl.BlockSpec((LANES, MIX_W), lambda b, c: (0, 0))
    state = pl.BlockSpec((1, RW_PAIRS, LANES, LANES), lambda b, c: (b, 0, 0, 0))
    in_specs = [tok(0), tok(1), tok(2),
                pl.BlockSpec((1, ts, LANES), lambda b, c: (b, c, 3 * MIX_W // LANES)),
                pl.BlockSpec((1, 1, RW_PROJ), lambda b, c: (b, 0, 0)),
                pl.BlockSpec((1, RW_PROJ), lambda b, c: (0, 0)),
                chan, chan, chan, chan, chan, chan, chan, lora, lora, state]
    row = lambda a: a.reshape(1, -1)
    return pl.pallas_call(
        functools.partial(_rwkv_kernel, ts=ts, cl=cl),
        grid=(bsz, t // ts),
        in_specs=in_specs,
        out_specs=[pl.BlockSpec((1, ts, MIX_W), lambda b, c: (b, c, 0)), state],
        out_shape=[jax.ShapeDtypeStruct((bsz, t, MIX_W), BF16),
                   jax.ShapeDtypeStruct((bsz, RW_PAIRS, LANES, LANES), F32)],
        scratch_shapes=[pltpu.VMEM((RW_PAIRS, LANES, LANES), F32), pltpu.VMEM((1, RW_PROJ), F32)],
        compiler_params=_params(("parallel", "arbitrary")),
        name="rwkv_core",
    )(p, p, p, p, prev_row, row(mu), row(w0), row(a0), row(k_k), row(k_a), row(r_k), row(ln_w), row(ln_b),
      wup, aup, z0)


def _pair_state_in(s):
    bsz = s.shape[0]
    st = jnp.swapaxes(s.astype(F32), -1, -2).reshape(bsz, RW_PAIRS, 2, RW_HD, RW_HD)
    z = st[:, :, :, :, None, :] * jnp.eye(2, dtype=F32)[None, None, :, None, :, None]
    return z.reshape(bsz, RW_PAIRS, LANES, LANES)


def _pair_state_out(z):
    bsz = z.shape[0]
    z6 = z.reshape(bsz, RW_PAIRS, 2, RW_HD, 2, RW_HD)
    st = jnp.stack([z6[:, :, 0, :, 0, :], z6[:, :, 1, :, 1, :]], axis=2)
    return jnp.swapaxes(st.reshape(bsz, RW_HEADS, RW_HD, RW_HD), -1, -2)


SUM_ROWS = 16


def _attn_stream_width(tq):
    return min(tq, 2 * LANES)


def _diff_attn_kernel(q_ref, k_ref, v_ref, lq1, lk1, lq2, lk2, sub_ref, o_ref, vbt,
                      *, tq, tk, t_kv, t_valid, past, lam_init):
    qi = pl.program_id(2)
    kb = k_ref.at[0]

    @pl.when(qi == 0)
    def _stage():
        ei = lax.broadcasted_iota(jnp.int32, (LANES, LANES), 0)
        ej = lax.broadcasted_iota(jnp.int32, (LANES, LANES), 1)
        eye = jnp.where(ei == ej, 1.0, 0.0).astype(BF16)

        def body(c, _):
            start = pl.multiple_of(c * tk, tk)
            vt = lax.dot_general(eye, v_ref[0, pl.ds(start, tk), :], (((1,), (1,)), ((), ())),
                                 preferred_element_type=F32)
            vbt[:LANES, pl.ds(start, tk)] = vt.astype(BF16)
            return 0

        lax.fori_loop(0, t_kv // tk, body, 0)
        ones_row = lax.broadcasted_iota(jnp.int32, (SUM_ROWS, t_kv), 0) == 0
        vbt[LANES:, :] = jnp.where(ones_row, 1.0, 0.0).astype(BF16)

    lam = (jnp.exp(jnp.sum(lq1[...] * lk1[...], axis=-1, keepdims=True))
           - jnp.exp(jnp.sum(lq2[...] * lk2[...], axis=-1, keepdims=True)) + lam_init)

    lane = lax.broadcasted_iota(jnp.int32, (tq, LANES), 1)
    q = q_ref[0].astype(F32) * (DF_HD ** -0.5 * LOG2_E)
    q1 = jnp.where(lane < DF_HD, q, 0.0).astype(BF16)
    q2 = jnp.where(lane >= DF_HD, q, 0.0).astype(BF16)

    q0 = past + qi * tq
    lim_first = jnp.minimum(((q0 >> CHUNK_SHIFT) + 1) << CHUNK_SHIFT, t_valid)
    lim_last = jnp.minimum((((q0 + tq - 1) >> CHUNK_SHIFT) + 1) << CHUNK_SHIFT, t_valid)
    n_full = lim_first // tk
    n_tot = (lim_last + tk - 1) // tk

    tw = _attn_stream_width(tq)
    n_blk = tq // tw
    n_str = 2 * n_blk
    q_str = [qh[i * tw:(i + 1) * tw] for qh in (q1, q2) for i in range(n_blk)]
    q_pos = q0 + lax.broadcasted_iota(jnp.int32, (1, tq), 1)
    q_lim = jnp.minimum(((q_pos >> CHUNK_SHIFT) + 1) << CHUNK_SHIFT, t_valid)
    lim_str = [q_lim[:, i * tw:(i + 1) * tw] for _ in range(2) for i in range(n_blk)]
    rows_full = [tk] * n_str

    def tile(j, rows):
        start = pl.multiple_of(j * tk, tk)
        uniq = {n: (kb[pl.ds(start, n), :], vbt[:, pl.ds(start, n)]) for n in sorted(set(rows)) if n}
        uniq[0] = (None, None)
        return start, [uniq[n][0] for n in rows], [uniq[n][1] for n in rows]

    def qk(kcs):
        return [None if kc is None else
                lax.dot_general(kc, qs, (((1,), (1,)), ((), ())), preferred_element_type=F32)
                for kc, qs in zip(kcs, q_str)]

    def softmax_update(s, m_old, start, masked):
        live = [x is not None for x in s]
        if masked:
            k_pos = {n: start + lax.broadcasted_iota(jnp.int32, (n, tw), 0)
                     for n in set(x.shape[0] for x in s if x is not None)}
            s = [jnp.where(k_pos[x.shape[0]] < lim, x, NEG_INF) if ok else None
                 for ok, x, lim in zip(live, s, lim_str)]
        m_new = [jnp.maximum(m, jnp.max(x, axis=0, keepdims=True)) if ok else m
                 for ok, m, x in zip(live, m_old, s)]
        corr = [jnp.exp2(m - mn) if ok else None for ok, m, mn in zip(live, m_old, m_new)]
        p = [jnp.exp2(x - mn) if ok else None for ok, x, mn in zip(live, s, m_new)]
        return m_new, corr, p

    def pv(vcts, p, acc_old, corr):
        return [a if x is None else a * c + jnp.dot(vct, x.astype(BF16), preferred_element_type=F32)
                for vct, a, c, x in zip(vcts, acc_old, corr, p)]

    def single(j, carry, masked, rows=rows_full):
        m_old, acc_old = carry
        start, kcs, vcts = tile(j, rows)
        m_new, corr, p = softmax_update(qk(kcs), m_old, start, masked)
        return tuple(m_new), tuple(pv(vcts, p, acc_old, corr))

    def pair(j2, carry):
        m_old, acc_old = carry
        start_a, kc_a, vct_a = tile(2 * j2, rows_full)
        start_b, kc_b, vct_b = tile(2 * j2 + 1, rows_full)
        s_a = qk(kc_a)
        s_b = qk(kc_b)
        m_a, corr_a, p_a = softmax_update(s_a, m_old, start_a, False)
        acc_a = pv(vct_a, p_a, acc_old, corr_a)
        m_b, corr_b, p_b = softmax_update(s_b, m_a, start_b, False)
        return tuple(m_b), tuple(pv(vct_b, p_b, acc_a, corr_b))

    carry = (tuple(jnp.full((1, tw), NEG_INF, F32) for _ in range(n_str)),
             tuple(jnp.zeros((LANES + SUM_ROWS, tw), F32) for _ in range(n_str)))
    n_pair = n_full // 2
    carry = lax.fori_loop(0, n_pair, pair, carry)
    carry = lax.fori_loop(2 * n_pair, n_full, functools.partial(single, masked=False), carry)
    if past == 0 and tq % tk == 0 and t_valid == t_kv:
        for dt in range(tq // tk):
            rows = [min(max((i + 1) * tw - dt * tk, 0), tk) for _ in range(2) for i in range(n_blk)]
            carry = single(n_full + dt, carry, True, rows)
    else:
        carry = lax.fori_loop(n_full, n_tot, functools.partial(single, masked=True), carry)
    _, acc_fin = carry
    cat = lambda xs: xs[0] if len(xs) == 1 else jnp.concatenate(xs, axis=1)
    l1 = cat([x[LANES:LANES + 1] for x in acc_fin[:n_blk]])
    l2 = cat([x[LANES:LANES + 1] for x in acc_fin[n_blk:]])
    acc1 = cat([x[:LANES] for x in acc_fin[:n_blk]])
    acc2 = cat([x[:LANES] for x in acc_fin[n_blk:]])
    o_t = acc1 * (1.0 / l1) - acc2 * (lam / l2)
    ms = jnp.mean(o_t * o_t, axis=0, keepdims=True)
    o_t = o_t * lax.rsqrt(ms + EPS)
    o_ref[0] = (o_t.T * sub_ref[...] * (1.0 - lam_init)).astype(o_ref.dtype)


def diff_attn(q, k_all, v_all, past, t_valid, lq1, lk1, lq2, lk2, subln, lam_init, tq, tk):
    bsz, t_q, _ = q.shape
    t_kv = k_all.shape[1]
    assert t_q % tq == 0 and t_kv % tk == 0 and t_kv % LANES == 0 and tq % LANES == 0
    kern = functools.partial(_diff_attn_kernel, tq=tq, tk=tk, t_kv=t_kv, t_valid=t_valid,
                             past=past, lam_init=lam_init)
    vec = pl.BlockSpec((1, DF_HD), lambda b, h, i: (0, 0))
    kv = pl.BlockSpec((1, t_kv, LANES), lambda b, h, i: (b, 0, h))
    row = lambda a: a.reshape(1, -1)
    return pl.pallas_call(
        kern,
        grid=(bsz, DF_HEADS, t_q // tq),
        in_specs=[pl.BlockSpec((1, tq, LANES), lambda b, h, i: (b, i, h)), kv, kv,
                  vec, vec, vec, vec, pl.BlockSpec((1, DF_VD), lambda b, h, i: (0, 0))],
        out_specs=pl.BlockSpec((1, tq, LANES), lambda b, h, i: (b, i, h)),
        out_shape=jax.ShapeDtypeStruct((bsz, t_q, MIX_W), BF16),
        scratch_shapes=[pltpu.VMEM((LANES + SUM_ROWS, t_kv), BF16)],
        compiler_params=_params(("parallel", "parallel", "arbitrary")),
        name="diff_attn",
    )(q, k_all, v_all, row(lq1), row(lk1), row(lq2), row(lk2), row(subln))


def _gate_out_kernel(x_ref, ymix_ref, gq_ref, mk_ref, mv_ref, wout_ref, nw_ref, *out_refs):
    gq = gq_ref[0]
    gate = gq[:, :BRANCH].astype(F32)
    heads = []
    for h in range(MEM_HEADS):
        sl = slice(h * MEM_HD, (h + 1) * MEM_HD)
        qh = gq[:, BRANCH + h * MEM_HD:BRANCH + (h + 1) * MEM_HD]
        s = _mm_nt(qh, mk_ref[0][:, sl]) * (MEM_HD ** -0.5)
        e = jnp.exp(s - jnp.max(s, axis=-1, keepdims=True))
        prob = e * (1.0 / jnp.sum(e, axis=-1, keepdims=True))
        heads.append(_mm(prob, mv_ref[0][:, sl]))
    branch = jnp.concatenate([ymix_ref[0].astype(F32)] + heads, axis=-1)
    act = (branch * (gate * _sigmoid(gate))).astype(BF16)
    xn = x_ref[0] + jnp.dot(act, wout_ref[...], preferred_element_type=F32)
    normed_ref = out_refs[-1]
    normed_ref[0] = (_rms(xn) * nw_ref[...]).astype(normed_ref.dtype)
    if len(out_refs) == 2:
        out_refs[0][0] = xn


def gate_out(x, ymix, gq, mk, mv, w_out, next_norm_w, last):
    bsz, t, d = x.shape
    bm = min(GATE_ROWS, t)
    assert t % bm == 0
    tokb = lambda w: pl.BlockSpec((1, bm, w), lambda b, i: (b, i, 0))
    mem = pl.BlockSpec((1, N_MEM, MEM_W), lambda b, i: (b, 0, 0))
    if last:
        out_specs, out_shape = tokb(d), jax.ShapeDtypeStruct((bsz, t, d), F32)
    else:
        out_specs = [tokb(d), tokb(d)]
        out_shape = [jax.ShapeDtypeStruct((bsz, t, d), F32), jax.ShapeDtypeStruct((bsz, t, d), BF16)]
    return pl.pallas_call(
        _gate_out_kernel,
        grid=(bsz, t // bm),
        in_specs=[tokb(d), tokb(MIX_W), tokb(BRANCH + MEM_W), mem, mem,
                  pl.BlockSpec((BRANCH, d), lambda b, i: (0, 0), pipeline_mode=pl.Buffered(1)),
                  pl.BlockSpec((1, d), lambda b, i: (0, 0))],
        out_specs=out_specs,
        out_shape=out_shape,
        compiler_params=_params(("parallel", "parallel")),
        name="gate_out",
    )(x, ymix, gq, mk, mv, w_out, next_norm_w.reshape(1, d))


def _mem_kv(mem, norm_w, wk, wv):
    bsz = mem.shape[0]
    hm = rmsnorm_bf16(mem.reshape(bsz * N_MEM, D_MODEL), norm_w)
    mk = proj(hm, wk.astype(BF16), MEM_W, F32)
    mv = proj(hm, wv.astype(BF16), MEM_W, F32)
    return mk.reshape(bsz, N_MEM, MEM_W), mv.reshape(bsz, N_MEM, MEM_W)


def _rwkv_layer(x, h, mk, mv, prev_row, s0, w_out, next_norm_w, w_in, mu, w0, w_up, a0, a_up,
                k_k, k_a, r_k, ln_w, ln_b, ts, cl):
    bsz, t, d = x.shape
    w_p = jnp.pad(w_in[:, :RW_PROJ], ((0, 0), (0, RW_PROJ_PAD - RW_PROJ))).astype(BF16)
    w_gq = jnp.concatenate([w_in[:, RW_PROJ + MEM_W:], w_in[:, RW_PROJ:RW_PROJ + MEM_W]], axis=1).astype(BF16)
    p = proj(h, w_p, RW_PROJ_PAD // 2, BF16).reshape(bsz, t, RW_PROJ_PAD)
    gq = proj(h, w_gq, (BRANCH + MEM_W) // 2, BF16).reshape(bsz, t, BRANCH + MEM_W)
    p_last = proj(h.reshape(bsz, t, d)[:, t - 1], w_p, RW_PROJ_PAD // 2, F32)[:, :RW_PROJ]
    zero = jnp.zeros((LORA, MIX_W), F32)
    wup = jnp.concatenate([w_up, zero], axis=0).astype(BF16)
    aup = jnp.concatenate([zero, a_up], axis=0).astype(BF16)
    y_mix, z_t = rwkv_core(p, prev_row.reshape(bsz, 1, RW_PROJ), _pair_state_in(s0), mu, w0, a0, k_k, k_a,
                           r_k.reshape(-1), ln_w, ln_b, wup, aup, ts, cl)
    x_new, h_new = gate_out(x, y_mix, gq, mk, mv, w_out.astype(BF16), next_norm_w, last=False)
    return x_new, h_new.reshape(bsz * t, d), _pair_state_out(z_t), p_last


def _round_up(n, m):
    return (n + m - 1) // m * m


def _diff_layer(x, h, mk, mv, k_past, v_past, layer_idx, w_out, final_w, w_in, lq1, lk1, lq2, lk2,
                subln, tq, tk):
    bsz, t, d = x.shape
    cols = lambda lo, hi: w_in[:, lo:hi].astype(BF16)
    q = proj(h, cols(0, MIX_W), MIX_W, BF16).reshape(bsz, t, MIX_W)
    k4, k = proj_heads(h, cols(MIX_W, 2 * MIX_W))
    v4, v = proj_heads(h, cols(2 * MIX_W, 3 * MIX_W))
    k, v = k.reshape(bsz, t, MIX_W), v.reshape(bsz, t, MIX_W)
    w_gq = jnp.concatenate([w_in[:, 3 * MIX_W + MEM_W:], w_in[:, 3 * MIX_W:3 * MIX_W + MEM_W]], axis=1).astype(BF16)
    gq = proj(h, w_gq, (BRANCH + MEM_W) // 2, BF16).reshape(bsz, t, BRANCH + MEM_W)
    past = 0 if k_past is None else k_past.shape[1]
    t_valid = past + t
    t_q, t_kv = _round_up(t, tq), _round_up(t_valid, tk)
    pad_rows = lambda a, n: a if n == 0 else jnp.pad(a, ((0, 0), (0, n), (0, 0)))
    k_all, v_all = k, v
    if k_past is not None:
        k_all = jnp.concatenate([k_past.reshape(bsz, past, MIX_W).astype(BF16), k], axis=1)
        v_all = jnp.concatenate([v_past.reshape(bsz, past, MIX_W).astype(BF16), v], axis=1)
    lam_init = 0.8 - 0.6 * math.exp(-0.3 * layer_idx)
    o = diff_attn(pad_rows(q, t_q - t), pad_rows(k_all, t_kv - t_valid), pad_rows(v_all, t_kv - t_valid),
                  past, t_valid, lq1, lk1, lq2, lk2, subln, lam_init, tq, tk)[:, :t]
    y = gate_out(x, o, gq, mk, mv, w_out.astype(BF16), final_w, last=True)
    return y, k4.reshape(bsz, t, DF_HEADS, 2 * DF_HD), v4.reshape(bsz, t, DF_HEADS, DF_VD)


def kernel(x_prompt, mem_prompt, x_sample, state_rwkv, state_shift, cache_k, cache_v, cache_mem_k, cache_mem_v, norm_w, mem_norm_w, w_mem_k, w_mem_v, w_out, final_norm_w, rw_in, rw_mu, rw_w0, rw_w_up, rw_a0, rw_a_up, rw_k_k, rw_k_a, rw_r_k, rw_ln_w, rw_ln_b, df_in, df_lq1, df_lk1, df_lq2, df_lk2, df_subln):
    bp, tp, d = x_prompt.shape
    bs, tsm, _ = x_sample.shape
    mem_s = lambda m: m.reshape(bs, N_MEM, MEM_W)
    mk0, mv0 = _mem_kv(mem_prompt, mem_norm_w[0], w_mem_k[0], w_mem_v[0])
    mk1, mv1 = _mem_kv(mem_prompt, mem_norm_w[1], w_mem_k[1], w_mem_v[1])

    rw = (rw_in[0], rw_mu[0], rw_w0[0], rw_w_up[0], rw_a0[0], rw_a_up[0], rw_k_k[0], rw_k_a[0], rw_r_k[0],
          rw_ln_w[0], rw_ln_b[0])
    hp = rmsnorm_bf16(x_prompt.reshape(bp * tp, d), norm_w[0])
    xp, hp, p_s, p_shift = _rwkv_layer(
        x_prompt, hp, mk0, mv0, jnp.zeros((bp, RW_PROJ), F32), jnp.zeros((bp, RW_HEADS, RW_HD, RW_HD), F32),
        w_out[0], norm_w[1], *rw, ts=RW_STEP, cl=RW_CHUNK)
    hs = rmsnorm_bf16(x_sample.reshape(bs * tsm, d), norm_w[0])
    xs, hs, s_s, s_shift = _rwkv_layer(
        x_sample, hs, mem_s(cache_mem_k[0]), mem_s(cache_mem_v[0]), state_shift[0], state_rwkv[0],
        w_out[0], norm_w[1], *rw, ts=tsm, cl=tsm)

    df = (df_in[0], df_lq1[0], df_lk1[0], df_lq2[0], df_lk2[0], df_subln[0])
    y_prompt, p_k, p_v = _diff_layer(xp, hp, mk1, mv1, None, None, 1, w_out[1], final_norm_w, *df,
                                     tq=ATTN_Q_TILE, tk=ATTN_K_TILE)
    t_kv_s = _round_up(cache_k.shape[2] + tsm, LANES)
    y_sample, s_k, s_v = _diff_layer(xs, hs, mem_s(cache_mem_k[1]), mem_s(cache_mem_v[1]), cache_k[0], cache_v[0], 1,
                                     w_out[1], final_norm_w, *df, tq=LANES, tk=t_kv_s)

    mem4 = lambda m: m.reshape(bp, N_MEM, MEM_HEADS, MEM_HD)
    return (y_prompt, y_sample, p_s[None], p_shift[None], p_k[None], p_v[None],
            jnp.stack([mem4(mk0), mem4(mk1)]), jnp.stack([mem4(mv0), mem4(mv1)]),
            s_s[None], s_shift[None], s_k[None], s_v[None])
```
